```python
import math
import jax, jax.numpy as jnp
from jax import lax
import numpy as np

D_MODEL = 2048
BATCH = 8
SEQ = 2048
DEPTH = 1

N_ATT_HEADS = 8
ATT_HEAD_DIM = D_MODEL // (4 * N_ATT_HEADS)
ATT_V_DIM = 2 * ATT_HEAD_DIM
ATT_WIDTH = N_ATT_HEADS * ATT_V_DIM
QK_COLS = N_ATT_HEADS * 2 * ATT_HEAD_DIM
ROPE_THETA = 10000.0
Q_BLOCK = 128
SGU_WIDTH = D_MODEL // 2
N_SGU_GROUPS = 8
SGU_GROUP_DIM = SGU_WIDTH // N_SGU_GROUPS
CHUNK = 128
N_BRANCHES = 2
GATE_COLS = N_BRANCHES * D_MODEL
IN_COLS = QK_COLS + QK_COLS + ATT_WIDTH + SGU_WIDTH + SGU_WIDTH + GATE_COLS
D_FF = 256 * ((8 * D_MODEL // 3 + 255) // 256)
CONV_WIDTH = 3
EPS = 1e-6

kernel_name = "hybrid_diffattn_sgu_convffn"


def rms_norm(x, g):
    xf = x.astype(jnp.float32)
    y = xf * lax.rsqrt(jnp.mean(xf * xf, axis=-1, keepdims=True) + EPS)
    return (y * g.astype(jnp.float32)).astype(x.dtype)


def layer_norm(x, g, b):
    xf = x.astype(jnp.float32)
    mu = jnp.mean(xf, axis=-1, keepdims=True)
    xc = xf - mu
    y = xc * lax.rsqrt(jnp.mean(xc * xc, axis=-1, keepdims=True) + EPS)
    return (y * g.astype(jnp.float32) + b.astype(jnp.float32)).astype(x.dtype)


def rope_tables(seq, dim):
    inv = jnp.exp(-math.log(ROPE_THETA) * jnp.arange(0, dim, 2, dtype=jnp.float32) / dim)
    ang = jnp.arange(seq, dtype=jnp.float32)[:, None] * inv[None, :]
    return jnp.cos(ang), jnp.sin(ang)


def apply_rope(x, cos, sin):
    x1, x2 = jnp.split(x.astype(jnp.float32), 2, axis=-1)
    c = cos[None, :, None, None, :]
    s = sin[None, :, None, None, :]
    return jnp.concatenate([x1 * c - x2 * s, x1 * s + x2 * c], axis=-1).astype(x.dtype)


def diff_attention(q, k, v, lam):
    seq = q.shape[1]
    scale = ATT_HEAD_DIM ** -0.5
    outs = []
    for blk in range(seq // Q_BLOCK):
        q0 = blk * Q_BLOCK
        kv_len = q0 + Q_BLOCK
        qb = q[:, q0:kv_len]
        kb = k[:, :kv_len]
        vb = v[:, :kv_len]
        s = jnp.einsum('bqhcd,bkhcd->bhcqk', qb, kb,
                       preferred_element_type=jnp.float32) * scale
        mask = (q0 + jnp.arange(Q_BLOCK))[:, None] >= jnp.arange(kv_len)[None, :]
        p = jax.nn.softmax(jnp.where(mask, s, -jnp.inf), axis=-1)
        a = p[:, :, 0] - lam * p[:, :, 1]
        outs.append(jnp.einsum('bhqk,bkhd->bqhd', a.astype(v.dtype), vb))
    return jnp.concatenate(outs, axis=1)


def spatial_gating(u, v, ln_g, ln_b, w_s, b_s):
    bsz, seq, _ = v.shape
    v = layer_norm(v, ln_g, ln_b)
    vc = v.reshape(bsz, seq // CHUNK, CHUNK, N_SGU_GROUPS, SGU_GROUP_DIM)
    causal = jnp.tril(jnp.ones((CHUNK, CHUNK), dtype=bool))
    w = jnp.where(causal[None], w_s, jnp.zeros_like(w_s)).astype(v.dtype)
    s = jnp.einsum('gtp,bnpgc->bntgc', w, vc) + b_s.T.astype(v.dtype)[None, None, :, :, None]
    return u * s.reshape(bsz, seq, SGU_WIDTH)


def causal_dwconv(x, w, b):
    ch = x.shape[-1]
    y = lax.conv_general_dilated(x, w[:, None, :].astype(x.dtype), window_strides=(1,),
                                 padding=[(CONV_WIDTH - 1, 0)],
                                 dimension_numbers=('NWC', 'WIO', 'NWC'),
                                 feature_group_count=ch)
    return y + b.astype(x.dtype)


def setup_inputs(seed: int = 0) -> dict:
    key = jax.random.key(seed)
    ks = jax.random.split(key, 24)
    f32 = jnp.float32
    nrm = lambda k, shape, scale: jax.random.normal(k, shape, f32) * scale
    L = DEPTH
    return {
        "x": nrm(ks[0], (BATCH, SEQ, D_MODEL), 1.0),
        "norm1_g": 1.0 + nrm(ks[1], (L, D_MODEL), 0.02),
        "w_in": nrm(ks[2], (L, D_MODEL, IN_COLS), D_MODEL ** -0.5),
        "b_gate": nrm(ks[3], (L, GATE_COLS), 0.02),
        "q_norm_g": 1.0 + nrm(ks[4], (L, ATT_HEAD_DIM), 0.02),
        "k_norm_g": 1.0 + nrm(ks[5], (L, ATT_HEAD_DIM), 0.02),
        "lambda_q1": nrm(ks[6], (L, ATT_HEAD_DIM), 0.1),
        "lambda_k1": nrm(ks[7], (L, ATT_HEAD_DIM), 0.1),
        "lambda_q2": nrm(ks[8], (L, ATT_HEAD_DIM), 0.1),
        "lambda_k2": nrm(ks[9], (L, ATT_HEAD_DIM), 0.1),
        "subln_g": 1.0 + nrm(ks[10], (L, ATT_V_DIM), 0.02),
        "sgu_norm_g": 1.0 + nrm(ks[11], (L, SGU_WIDTH), 0.02),
        "sgu_norm_b": nrm(ks[12], (L, SGU_WIDTH), 0.02),
        "sgu_w": nrm(ks[13], (L, N_SGU_GROUPS, CHUNK, CHUNK), 0.5 * CHUNK ** -0.5),
        "sgu_b": 1.0 + nrm(ks[14], (L, N_SGU_GROUPS, CHUNK), 0.02),
        "w_att_out": nrm(ks[15], (L, ATT_WIDTH, D_MODEL), ATT_WIDTH ** -0.5),
        "w_sgu_out": nrm(ks[16], (L, SGU_WIDTH, D_MODEL), SGU_WIDTH ** -0.5),
        "w_out": nrm(ks[17], (L, D_MODEL, D_MODEL), D_MODEL ** -0.5),
        "norm2_g": 1.0 + nrm(ks[18], (L, D_MODEL), 0.02),
        "w_up": nrm(ks[19], (L, D_MODEL, 2 * D_FF), D_MODEL ** -0.5),
        "conv_w": nrm(ks[20], (L, CONV_WIDTH, 2 * D_FF), CONV_WIDTH ** -0.5),
        "conv_b": nrm(ks[21], (L, 2 * D_FF), 0.02),
        "w_down": nrm(ks[22], (L, D_FF, D_MODEL), D_FF ** -0.5),
    }


def reference(x, norm1_g, w_in, b_gate, q_norm_g, k_norm_g, lambda_q1, lambda_k1,
              lambda_q2, lambda_k2, subln_g, sgu_norm_g, sgu_norm_b, sgu_w, sgu_b,
              w_att_out, w_sgu_out, w_out, norm2_g, w_up, conv_w, conv_b, w_down):
    bsz, seq, _ = x.shape
    cos, sin = rope_tables(seq, ATT_HEAD_DIM)
    split_points = [QK_COLS, 2 * QK_COLS, 2 * QK_COLS + ATT_WIDTH,
                    2 * QK_COLS + ATT_WIDTH + SGU_WIDTH,
                    2 * QK_COLS + ATT_WIDTH + 2 * SGU_WIDTH]
    h = x
    for l in range(DEPTH):
        a = rms_norm(h, norm1_g[l])
        z = a @ w_in[l]
        q, k, v, u, sv, gates = jnp.split(z, split_points, axis=-1)

        q = apply_rope(rms_norm(q.reshape(bsz, seq, N_ATT_HEADS, 2, ATT_HEAD_DIM), q_norm_g[l]), cos, sin)
        k = apply_rope(rms_norm(k.reshape(bsz, seq, N_ATT_HEADS, 2, ATT_HEAD_DIM), k_norm_g[l]), cos, sin)
        v = v.reshape(bsz, seq, N_ATT_HEADS, ATT_V_DIM)
        lam_init = 0.8 - 0.6 * math.exp(-0.3 * l)
        lam = (jnp.exp(jnp.sum(lambda_q1[l].astype(jnp.float32) * lambda_k1[l].astype(jnp.float32)))
               - jnp.exp(jnp.sum(lambda_q2[l].astype(jnp.float32) * lambda_k2[l].astype(jnp.float32)))
               + lam_init)
        o = diff_attention(q, k, v, lam)
        o = rms_norm(o, subln_g[l]) * (1.0 - lam_init)
        att_branch = o.reshape(bsz, seq, ATT_WIDTH) @ w_att_out[l]

        sgu = spatial_gating(jax.nn.gelu(u, approximate=False), jax.nn.gelu(sv, approximate=False),
                             sgu_norm_g[l], sgu_norm_b[l], sgu_w[l], sgu_b[l])
        sgu_branch = sgu @ w_sgu_out[l]

        g_att, g_sgu = jnp.split(jax.nn.sigmoid(gates + b_gate[l]), N_BRANCHES, axis=-1)
        h = h + (g_att * att_branch + g_sgu * sgu_branch) @ w_out[l]

        c = rms_norm(h, norm2_g[l])
        up = causal_dwconv(c @ w_up[l], conv_w[l], conv_b[l])
        gate_ff, val_ff = jnp.split(up, 2, axis=-1)
        h = h + (jax.nn.gelu(gate_ff, approximate=False) * val_ff) @ w_down[l]
    return h
```

```python
import functools
import math

import jax
import jax.numpy as jnp
from jax import lax
from jax.experimental import pallas as pl
from jax.experimental.pallas import tpu as pltpu

F32 = jnp.float32
BF16 = jnp.bfloat16

D_MODEL = 2048
N_HEADS = 8
HEAD_DIM = 64
V_DIM = 2 * HEAD_DIM
SEC = 1024
N_SEC = 9
N_GATE_SEC = 4
Z_GATT, Z_GSGU, Z_Q, Z_K, Z_V, Z_U, Z_SV = 0, 2, 4, 5, 6, 7, 8
ROPE_THETA = 10000.0
CHUNK = 128
N_GROUPS = 8
GROUP_DIM = 128
D_FF = 5632
CONV_WIDTH = 3
EPS = 1e-6
LAM_INIT = 0.8 - 0.6 * math.exp(-0.3 * 0)
SQRT_HALF = math.sqrt(0.5)

V7X_VMEM_BYTES = 64 * 1024 * 1024
LANES = 128
BF16_SUBLANES = 16

IN_TM = 1024
IN_RC = 256
ATT_TQ = 256
MRG_TM = 256
FFN_TM = 512
FFN_TF = 512
FFN_HALO = BF16_SUBLANES


def _gelu(x):
    return 0.5 * x * (1.0 + lax.erf(x * SQRT_HALF))


def _vmem_limit(nbytes):
    return int(min(V7X_VMEM_BYTES - (4 << 20), nbytes))


def _in_proj_kernel(x_ref, g1_ref, w_ref, bg_ref, qg_ref, kg_ref, cos_ref, sin_ref, seg_ref,
                    lng_ref, lnb_ref, o_ref, a_ref):
    j = pl.program_id(1)
    n_chunks = IN_TM // IN_RC

    @pl.when(j == 0)
    def _():
        for r in range(n_chunks):
            rows = pl.ds(r * IN_RC, IN_RC)
            xs = x_ref[rows, :]
            ms = jnp.mean(xs * xs, axis=-1, keepdims=True)
            a_ref[rows, :] = (xs * lax.rsqrt(ms + EPS) * g1_ref[...]).astype(BF16)

    def for_chunks(epilogue):
        for r in range(n_chunks):
            rows = pl.ds(r * IN_RC, IN_RC)
            z = jnp.dot(a_ref[rows, :], w_ref[...], preferred_element_type=F32)
            o_ref[rows, :] = epilogue(z, rows).astype(BF16)

    def qk_epilogue(gain_ref, scale):
        def ep(z, rows):
            sq = (z * z).astype(BF16)
            ss = jnp.concatenate(
                [jnp.dot(sq[:, c * 256:(c + 1) * 256], seg_ref[...], preferred_element_type=F32)
                 for c in range(SEC // 256)], axis=1)
            rinv = lax.rsqrt(ss * (1.0 / HEAD_DIM) + EPS)
            zg = z * gain_ref[...]
            lane = lax.broadcasted_iota(jnp.int32, zg.shape, 1)
            partner = jnp.where((lane & 32) == 0,
                                pltpu.roll(zg, SEC - 32, axis=1), pltpu.roll(zg, 32, axis=1))
            cos = jnp.concatenate([cos_ref[rows, :]] * N_HEADS, axis=1)
            sin = jnp.concatenate([sin_ref[rows, :]] * N_HEADS, axis=1)
            out = (zg * cos + partner * sin) * rinv
            return out * scale if scale != 1.0 else out
        return ep

    @pl.when(j == 0)
    def _():
        for_chunks(qk_epilogue(qg_ref, HEAD_DIM ** -0.5))

    @pl.when(j == 1)
    def _():
        for_chunks(qk_epilogue(kg_ref, 1.0))

    @pl.when(j == 2)
    def _():
        for_chunks(lambda z, rows: z)

    @pl.when(j == 3)
    def _():
        for_chunks(lambda z, rows: _gelu(z))

    @pl.when(j == 4)
    def _():
        def ep(z, rows):
            gl = _gelu(z)
            mu = jnp.mean(gl, axis=-1, keepdims=True)
            xc = gl - mu
            var = jnp.mean(xc * xc, axis=-1, keepdims=True)
            return xc * lax.rsqrt(var + EPS) * lng_ref[...] + lnb_ref[...]
        for_chunks(ep)

    @pl.when(j >= 5)
    def _():
        for_chunks(lambda z, rows: jax.nn.sigmoid(z + bg_ref[...]))


def _in_proj(x2, norm1_g, w_in, b_gate, qg_t, kg_t, cos_t, sin_t, seg, ln_g, ln_b, seq):
    m = x2.shape[0]
    blocks_per_seq = seq // IN_TM
    row = lambda i, j: (i, 0)
    const = lambda i, j: (0, 0)
    vmem = (2 * IN_TM * D_MODEL * 4 + 2 * D_MODEL * SEC * 2 + IN_TM * D_MODEL * 2
            + 2 * IN_TM * SEC * 2 + 12 * IN_RC * SEC * 4 + (4 << 20))
    return pl.pallas_call(
        _in_proj_kernel,
        name="in_proj",
        grid=(m // IN_TM, N_SEC),
        in_specs=[
            pl.BlockSpec((IN_TM, D_MODEL), row),
            pl.BlockSpec((1, D_MODEL), const),
            pl.BlockSpec((D_MODEL, SEC), lambda i, j: (0, j)),
            pl.BlockSpec((1, SEC), lambda i, j: (0, jnp.maximum(j - 5, 0))),
            pl.BlockSpec((1, SEC), const),
            pl.BlockSpec((1, SEC), const),
            pl.BlockSpec((IN_TM, LANES), lambda i, j: (i % blocks_per_seq, 0)),
            pl.BlockSpec((IN_TM, LANES), lambda i, j: (i % blocks_per_seq, 0)),
            pl.BlockSpec((256, 256), const),
            pl.BlockSpec((1, SEC), const),
            pl.BlockSpec((1, SEC), const),
        ],
        out_specs=pl.BlockSpec((IN_TM, SEC), lambda i, j: (i, (j + N_GATE_SEC) % N_SEC)),
        out_shape=jax.ShapeDtypeStruct((m, N_SEC * SEC), BF16),
        scratch_shapes=[pltpu.VMEM((IN_TM, D_MODEL), BF16)],
        compiler_params=pltpu.CompilerParams(
            dimension_semantics=("parallel", "arbitrary"), vmem_limit_bytes=_vmem_limit(vmem)),
    )(x2, norm1_g, w_in, b_gate, qg_t, kg_t, cos_t, sin_t, seg, ln_g, ln_b)


def _attn_kernel(lq1_ref, lk1_ref, lq2_ref, lk2_ref, subg_ref, q_ref, k_ref, v_ref, o_ref, *, seq):
    tq = ATT_TQ
    lam = (jnp.exp(jnp.sum(lq1_ref[...] * lk1_ref[...], axis=-1, keepdims=True))
           - jnp.exp(jnp.sum(lq2_ref[...] * lk2_ref[...], axis=-1, keepdims=True)) + LAM_INIT)
    lane = lax.broadcasted_iota(jnp.int32, (tq, V_DIM), 1)
    row = lax.broadcasted_iota(jnp.int32, (2 * tq, tq), 0)
    col = lax.broadcasted_iota(jnp.int32, (2 * tq, tq), 1)
    causal = jnp.where(row >= tq, row - tq, row) >= col
    nt_dims = (((1,), (1,)), ((), ()))

    for qi in range(seq // tq):
        q = q_ref[0, pl.ds(qi * tq, tq), :]
        zero = jnp.zeros_like(q)
        qs = jnp.concatenate([jnp.where(lane < HEAD_DIM, q, zero), jnp.where(lane >= HEAD_DIM, q, zero)], axis=0)

        def step(kj, carry, masked):
            m, l, acc = carry
            start = kj * tq
            if not isinstance(start, int):
                start = pl.multiple_of(start, tq)
            k = k_ref[0, pl.ds(start, tq), :]
            v = v_ref[0, pl.ds(start, tq), :]
            s = lax.dot_general(qs, k, nt_dims, preferred_element_type=F32)
            if masked:
                s = jnp.where(causal, s, -jnp.inf)
            m_new = jnp.maximum(m, jnp.max(s, axis=-1, keepdims=True))
            alpha = jnp.exp(m - m_new)
            p = jnp.exp(s - m_new)
            l = alpha * l + jnp.sum(p, axis=-1, keepdims=True)
            acc = alpha * acc + jnp.dot(p.astype(BF16), v, preferred_element_type=F32)
            return m_new, l, acc

        carry = (jnp.full((2 * tq, 1), -jnp.inf, F32), jnp.zeros((2 * tq, 1), F32),
                 jnp.zeros((2 * tq, V_DIM), F32))
        if qi > 0:
            carry = lax.fori_loop(0, qi, functools.partial(step, masked=False), carry)
        _, l, acc = step(qi, carry, True)
        o = acc[:tq] / l[:tq] - lam * (acc[tq:] / l[tq:])
        o = o * lax.rsqrt(jnp.mean(o * o, axis=-1, keepdims=True) + EPS) * subg_ref[...]
        o_ref[0, pl.ds(qi * tq, tq), :] = (o * (1.0 - LAM_INIT)).astype(BF16)


def _attention(z3, lq1, lk1, lq2, lk2, subln_g):
    b, seq, _ = z3.shape
    small = lambda n: pl.BlockSpec((1, n), lambda bi, h: (0, 0))
    blk = lambda off: pl.BlockSpec((1, seq, V_DIM), lambda bi, h: (bi, 0, off + h))
    return pl.pallas_call(
        functools.partial(_attn_kernel, seq=seq),
        name="attn",
        grid=(b, N_HEADS),
        in_specs=[small(HEAD_DIM)] * 4 + [small(V_DIM), blk(Z_Q * N_HEADS), blk(Z_K * N_HEADS), blk(Z_V * N_HEADS)],
        out_specs=pl.BlockSpec((1, seq, V_DIM), lambda bi, h: (bi, 0, h)),
        out_shape=jax.ShapeDtypeStruct((b, seq, N_HEADS * V_DIM), BF16),
        compiler_params=pltpu.CompilerParams(
            dimension_semantics=("parallel", "parallel"), vmem_limit_bytes=_vmem_limit(32 << 20)),
    )(lq1, lk1, lq2, lk2, subln_g, z3, z3, z3)


def _merge_kernel(x_ref, o_ref, u_ref, sv_ref, ga_ref, gs_ref, wsp_ref, bsp_ref, wa_ref, ws_ref, wo_ref,
                  g2_ref, h_ref, c_ref, sgu_ref):
    t_idx = lax.broadcasted_iota(jnp.int32, (CHUNK, CHUNK), 0)
    p_idx = lax.broadcasted_iota(jnp.int32, (CHUNK, CHUNK), 1)
    for g in range(N_GROUPS):
        cols = slice(g * GROUP_DIM, (g + 1) * GROUP_DIM)
        w = jnp.where(t_idx >= p_idx, wsp_ref[g], 0.0).astype(BF16)
        for c in range(MRG_TM // CHUNK):
            rows = pl.ds(c * CHUNK, CHUNK)
            s = jnp.dot(w, sv_ref[rows, cols], preferred_element_type=F32) + bsp_ref[:, cols]
            sgu_ref[rows, cols] = (u_ref[rows, cols].astype(F32) * s).astype(BF16)
    att_b = jnp.dot(o_ref[...], wa_ref[...], preferred_element_type=F32)
    sgu_b = jnp.dot(sgu_ref[...], ws_ref[...], preferred_element_type=F32)
    mix = (ga_ref[...].astype(F32) * att_b + gs_ref[...].astype(F32) * sgu_b).astype(BF16)
    h = x_ref[...] + jnp.dot(mix, wo_ref[...], preferred_element_type=F32)
    h_ref[...] = h
    ms = jnp.mean(h * h, axis=-1, keepdims=True)
    c_ref[...] = (h * lax.rsqrt(ms + EPS) * g2_ref[...]).astype(BF16)


def _merge(x2, o2, z2, sgu_w, bs_full, w_att_out, w_sgu_out, w_out, norm2_g):
    m = x2.shape[0]
    row = lambda i: (i, 0)
    const2 = lambda i: (0, 0)
    resident = functools.partial(pl.BlockSpec, pipeline_mode=pl.Buffered(1))
    zsec = lambda first, n: pl.BlockSpec((MRG_TM, n * SEC), lambda i: (i, first // n))
    vmem = ((SEC * D_MODEL * 2 + D_MODEL * D_MODEL) * 2
            + 2 * MRG_TM * (D_MODEL * 4 + 3 * SEC * 2 + 2 * D_MODEL * 2 + D_MODEL * 4 + D_MODEL * 2)
            + 6 * MRG_TM * D_MODEL * 4 + (6 << 20))
    return pl.pallas_call(
        _merge_kernel,
        name="merge",
        grid=(m // MRG_TM,),
        in_specs=[
            pl.BlockSpec((MRG_TM, D_MODEL), row),
            pl.BlockSpec((MRG_TM, SEC), row),
            zsec(Z_U, 1), zsec(Z_SV, 1), zsec(Z_GATT, 2), zsec(Z_GSGU, 2),
            resident((N_GROUPS, CHUNK, CHUNK), lambda i: (0, 0, 0)),
            resident((CHUNK, SEC), const2),
            resident((SEC, D_MODEL), const2),
            resident((SEC, D_MODEL), const2),
            resident((D_MODEL, D_MODEL), const2),
            pl.BlockSpec((1, D_MODEL), const2),
        ],
        out_specs=[pl.BlockSpec((MRG_TM, D_MODEL), row), pl.BlockSpec((MRG_TM, D_MODEL), row)],
        out_shape=[jax.ShapeDtypeStruct((m, D_MODEL), F32), jax.ShapeDtypeStruct((m, D_MODEL), BF16)],
        scratch_shapes=[pltpu.VMEM((MRG_TM, SEC), BF16)],
        compiler_params=pltpu.CompilerParams(
            dimension_semantics=("parallel",), vmem_limit_bytes=_vmem_limit(vmem)),
    )(x2, o2, z2, z2, z2, z2, sgu_w, bs_full, w_att_out, w_sgu_out, w_out, norm2_g)


def _ffn_kernel(c_ref, halo_ref, h_ref, wg_ref, wv_ref, cwg_ref, cwv_ref, cbg_ref, cbv_ref, wd_ref,
                o_ref, cext_ref, ug_ref, uv_ref, *, blocks_per_seq):
    i = pl.program_id(0)
    j = pl.program_id(1)

    @pl.when(j == 0)
    def _():
        halo = halo_ref[...]
        cext_ref[pl.ds(0, FFN_HALO), :] = jnp.where(i % blocks_per_seq == 0, jnp.zeros_like(halo), halo)
        cext_ref[pl.ds(FFN_HALO, FFN_TM), :] = c_ref[...]

    ug_ref[...] = jnp.dot(cext_ref[...], wg_ref[...], preferred_element_type=F32)
    uv_ref[...] = jnp.dot(cext_ref[...], wv_ref[...], preferred_element_type=F32)

    def conv(u_ref, w_ref, b_ref):
        acc = b_ref[...] + w_ref[pl.ds(CONV_WIDTH - 1, 1), :] * u_ref[pl.ds(FFN_HALO, FFN_TM), :]
        for tap in range(CONV_WIDTH - 1):
            shift = CONV_WIDTH - 1 - tap
            acc = acc + w_ref[pl.ds(tap, 1), :] * u_ref[pl.ds(FFN_HALO - shift, FFN_TM), :]
        return acc

    act = (_gelu(conv(ug_ref, cwg_ref, cbg_ref)) * conv(uv_ref, cwv_ref, cbv_ref)).astype(BF16)
    part = jnp.dot(act, wd_ref[...], preferred_element_type=F32)

    @pl.when(j == 0)
    def _():
        o_ref[...] = h_ref[...] + part

    @pl.when(j > 0)
    def _():
        o_ref[...] = o_ref[...] + part


def _ffn(c2, h2, w_up, conv_w, conv_b, w_down, seq):
    m = c2.shape[0]
    nf = D_FF // FFN_TF
    blocks_per_seq = seq // FFN_TM
    halo_blocks = FFN_TM // FFN_HALO
    row = lambda i, j: (i, 0)
    gate_col = lambda i, j: (0, j)
    val_col = lambda i, j: (0, nf + j)
    vmem = (2 * FFN_TM * D_MODEL * 2 + (FFN_TM + FFN_HALO) * D_MODEL * 2 + 4 * FFN_TM * D_MODEL * 4
            + 4 * D_MODEL * FFN_TF * 2 + 2 * FFN_TF * D_MODEL * 2
            + 2 * (FFN_TM + FFN_HALO) * FFN_TF * 4 + 8 * FFN_TM * FFN_TF * 4 + (6 << 20))
    return pl.pallas_call(
        functools.partial(_ffn_kernel, blocks_per_seq=blocks_per_seq),
        name="ffn",
        grid=(m // FFN_TM, nf),
        in_specs=[
            pl.BlockSpec((FFN_TM, D_MODEL), row),
            pl.BlockSpec((FFN_HALO, D_MODEL), lambda i, j: (jnp.maximum(i * halo_blocks - 1, 0), 0)),
            pl.BlockSpec((FFN_TM, D_MODEL), row),
            pl.BlockSpec((D_MODEL, FFN_TF), gate_col),
            pl.BlockSpec((D_MODEL, FFN_TF), val_col),
            pl.BlockSpec((CONV_WIDTH, FFN_TF), gate_col),
            pl.BlockSpec((CONV_WIDTH, FFN_TF), val_col),
            pl.BlockSpec((1, FFN_TF), gate_col),
            pl.BlockSpec((1, FFN_TF), val_col),
            pl.BlockSpec((FFN_TF, D_MODEL), lambda i, j: (j, 0)),
        ],
        out_specs=pl.BlockSpec((FFN_TM, D_MODEL), row),
        out_shape=jax.ShapeDtypeStruct((m, D_MODEL), F32),
        scratch_shapes=[
            pltpu.VMEM((FFN_TM + FFN_HALO, D_MODEL), BF16),
            pltpu.VMEM((FFN_TM + FFN_HALO, FFN_TF), F32),
            pltpu.VMEM((FFN_TM + FFN_HALO, FFN_TF), F32),
        ],
        compiler_params=pltpu.CompilerParams(
            dimension_semantics=("parallel", "arbitrary"), vmem_limit_bytes=_vmem_limit(vmem)),
    )(c2, c2, h2, w_up, w_up, conv_w, conv_w, conv_b, conv_b, w_down)


def _rope_tables(seq):
    inv = jnp.exp(-math.log(ROPE_THETA) * jnp.arange(0, HEAD_DIM, 2, dtype=F32) / HEAD_DIM)
    ang = jnp.arange(seq, dtype=F32)[:, None] * inv[None, :]
    cos, sin = jnp.cos(ang), jnp.sin(ang)
    return jnp.tile(cos, (1, 4)), jnp.tile(jnp.concatenate([-sin, sin], axis=1), (1, 2))


def kernel(x, norm1_g, w_in, b_gate, q_norm_g, k_norm_g, lambda_q1, lambda_k1, lambda_q2, lambda_k2, subln_g, sgu_norm_g, sgu_norm_b, sgu_w, sgu_b, w_att_out, w_sgu_out, w_out, norm2_g, w_up, conv_w, conv_b, w_down):
    bsz, seq, d = x.shape
    assert d == D_MODEL and w_in.shape[0] == 1
    assert seq % IN_TM == 0 and seq % FFN_TM == 0 and seq % ATT_TQ == 0 and seq % MRG_TM == 0
    m = bsz * seq
    x2 = x.reshape(m, d)
    cos_t, sin_t = _rope_tables(seq)
    seg = (jnp.arange(256)[:, None] // HEAD_DIM == jnp.arange(256)[None, :] // HEAD_DIM).astype(BF16)
    tile_heads = lambda g: jnp.tile(g, (1, SEC // HEAD_DIM))
    bs_full = jnp.repeat(sgu_b[0].T, GROUP_DIM, axis=1)

    z2 = _in_proj(x2, norm1_g, w_in[0].astype(BF16), b_gate, tile_heads(q_norm_g), tile_heads(k_norm_g),
                  cos_t, sin_t, seg, sgu_norm_g, sgu_norm_b, seq)
    o3 = _attention(z2.reshape(bsz, seq, N_SEC * SEC), lambda_q1, lambda_k1, lambda_q2, lambda_k2, subln_g)
    h2, c2 = _merge(x2, o3.reshape(m, SEC), z2, sgu_w[0], bs_full, w_att_out[0].astype(BF16),
                    w_sgu_out[0].astype(BF16), w_out[0].astype(BF16), norm2_g)
    out = _ffn(c2, h2, w_up[0].astype(BF16), conv_w[0], conv_b, w_down[0].astype(BF16), seq)
    return out.reshape(bsz, seq, d)
```

```python
import functools
import math

import jax
import jax.numpy as jnp
from jax import lax
from jax.experimental import pallas as pl
from jax.experimental.pallas import tpu as pltpu

F32 = jnp.float32
BF16 = jnp.bfloat16

D_MODEL = 2048
N_HEADS = 8
HEAD_DIM = 64
V_DIM = 2 * HEAD_DIM
SEC = 1024
N_SEC = 9
N_GATE_SEC = 4
Z_GATT, Z_GSGU, Z_Q, Z_K, Z_V, Z_U, Z_SV = 0, 2, 4, 5, 6, 7, 8
ROPE_THETA = 10000.0
CHUNK = 128
N_GROUPS = 8
GROUP_DIM = 128
D_FF = 5632
CONV_WIDTH = 3
EPS = 1e-6
LAM_INIT = 0.8 - 0.6 * math.exp(-0.3 * 0)
SQRT_HALF = math.sqrt(0.5)
LOG2_E = math.log2(math.e)

V7X_VMEM_BYTES = 64 * 1024 * 1024
LANES = 128
BF16_SUBLANES = 16

IN_TM = 1024
IN_RC = 256
ATT_TQ = 256
ATT_RC = 16
MRG_TM = 256
FFN_TM = 1024
FFN_TF = 512
FFN_SUB = 256
FFN_RC = 256
FFN_HALO = BF16_SUBLANES


def _gelu(x):
    return 0.5 * x * (1.0 + lax.erf(x * SQRT_HALF))


def _vmem_limit(nbytes):
    return int(min(V7X_VMEM_BYTES - (4 << 20), nbytes))


def _in_proj_kernel(x_ref, g1_ref, w_ref, bg_ref, qg_ref, kg_ref, cos_ref, sin_ref, seg_ref,
                    lng_ref, lnb_ref, o_ref, a_ref):
    j = pl.program_id(1)
    n_chunks = IN_TM // IN_RC

    @pl.when(j == 0)
    def _():
        for r in range(n_chunks):
            rows = pl.ds(r * IN_RC, IN_RC)
            xs = x_ref[rows, :]
            ms = jnp.mean(xs * xs, axis=-1, keepdims=True)
            a_ref[rows, :] = (xs * lax.rsqrt(ms + EPS) * g1_ref[...]).astype(BF16)

    def for_chunks(epilogue):
        for r in range(n_chunks):
            rows = pl.ds(r * IN_RC, IN_RC)
            z = jnp.dot(a_ref[rows, :], w_ref[...], preferred_element_type=F32)
            o_ref[rows, :] = epilogue(z, rows).astype(BF16)

    def qk_epilogue(gain_ref, scale):
        def ep(z, rows):
            sq = (z * z).astype(BF16)
            ss = jnp.concatenate(
                [jnp.dot(sq[:, c * 256:(c + 1) * 256], seg_ref[...], preferred_element_type=F32)
                 for c in range(SEC // 256)], axis=1)
            rinv = lax.rsqrt(ss * (1.0 / HEAD_DIM) + EPS)
            zg = z * gain_ref[...]
            lane = lax.broadcasted_iota(jnp.int32, zg.shape, 1)
            partner = jnp.where((lane & 32) == 0,
                                pltpu.roll(zg, SEC - 32, axis=1), pltpu.roll(zg, 32, axis=1))
            cos = jnp.concatenate([cos_ref[rows, :]] * N_HEADS, axis=1)
            sin = jnp.concatenate([sin_ref[rows, :]] * N_HEADS, axis=1)
            out = (zg * cos + partner * sin) * rinv
            return out * scale if scale != 1.0 else out
        return ep

    @pl.when(j == 0)
    def _():
        for_chunks(qk_epilogue(qg_ref, HEAD_DIM ** -0.5 * LOG2_E))

    @pl.when(j == 1)
    def _():
        for_chunks(qk_epilogue(kg_ref, 1.0))

    @pl.when(j == 2)
    def _():
        for_chunks(lambda z, rows: z)

    @pl.when(j == 3)
    def _():
        for_chunks(lambda z, rows: _gelu(z))

    @pl.when(j == 4)
    def _():
        def ep(z, rows):
            gl = _gelu(z)
            mu = jnp.mean(gl, axis=-1, keepdims=True)
            xc = gl - mu
            var = jnp.mean(xc * xc, axis=-1, keepdims=True)
            return xc * lax.rsqrt(var + EPS) * lng_ref[...] + lnb_ref[...]
        for_chunks(ep)

    @pl.when(j >= 5)
    def _():
        for_chunks(lambda z, rows: jax.nn.sigmoid(z + bg_ref[...]))


def _in_proj(x2, norm1_g, w_in, b_gate, qg_t, kg_t, cos_t, sin_t, seg, ln_g, ln_b, seq):
    m = x2.shape[0]
    blocks_per_seq = seq // IN_TM
    row = lambda i, j: (i, 0)
    const = lambda i, j: (0, 0)
    vmem = (2 * IN_TM * D_MODEL * 4 + 2 * D_MODEL * SEC * 2 + IN_TM * D_MODEL * 2
            + 2 * IN_TM * SEC * 2 + 12 * IN_RC * SEC * 4 + (4 << 20))
    return pl.pallas_call(
        _in_proj_kernel,
        name="in_proj",
        grid=(m // IN_TM, N_SEC),
        in_specs=[
            pl.BlockSpec((IN_TM, D_MODEL), row),
            pl.BlockSpec((1, D_MODEL), const),
            pl.BlockSpec((D_MODEL, SEC), lambda i, j: (0, j)),
            pl.BlockSpec((1, SEC), lambda i, j: (0, jnp.maximum(j - 5, 0))),
            pl.BlockSpec((1, SEC), const),
            pl.BlockSpec((1, SEC), const),
            pl.BlockSpec((IN_TM, LANES), lambda i, j: (i % blocks_per_seq, 0)),
            pl.BlockSpec((IN_TM, LANES), lambda i, j: (i % blocks_per_seq, 0)),
            pl.BlockSpec((256, 256), const),
            pl.BlockSpec((1, SEC), const),
            pl.BlockSpec((1, SEC), const),
        ],
        out_specs=pl.BlockSpec((IN_TM, SEC), lambda i, j: (i, (j + N_GATE_SEC) % N_SEC)),
        out_shape=jax.ShapeDtypeStruct((m, N_SEC * SEC), BF16),
        scratch_shapes=[pltpu.VMEM((IN_TM, D_MODEL), BF16)],
        compiler_params=pltpu.CompilerParams(
            dimension_semantics=("parallel", "arbitrary"), vmem_limit_bytes=_vmem_limit(vmem)),
    )(x2, norm1_g, w_in, b_gate, qg_t, kg_t, cos_t, sin_t, seg, ln_g, ln_b)


def _attn_kernel(lq1_ref, lk1_ref, lq2_ref, lk2_ref, subg_ref, q_ref, k_ref, v_ref, o_ref,
                 s_ref, p_ref, vext_ref, *, seq):
    tq = ATT_TQ
    lam = (jnp.exp(jnp.sum(lq1_ref[...] * lk1_ref[...], axis=-1, keepdims=True))
           - jnp.exp(jnp.sum(lq2_ref[...] * lk2_ref[...], axis=-1, keepdims=True)) + LAM_INIT)
    rc = ATT_RC
    lane = lax.broadcasted_iota(jnp.int32, (tq, V_DIM), 1)
    row_i = lax.broadcasted_iota(jnp.int32, (rc, tq), 0)
    col_i = lax.broadcasted_iota(jnp.int32, (rc, tq), 1)
    nt_dims = (((1,), (1,)), ((), ()))

    vext_ref[:, pl.ds(0, V_DIM)] = v_ref[0]
    vext_ref[:, pl.ds(V_DIM, V_DIM)] = jnp.ones((seq, V_DIM), BF16)

    for qi in range(seq // tq):
        slot = qi % 2
        off = qi * tq
        kv_len = off + tq
        q = q_ref[0, pl.ds(off, tq), :]
        zero = jnp.zeros_like(q)
        qs = jnp.concatenate([jnp.where(lane < HEAD_DIM, q, zero), jnp.where(lane >= HEAD_DIM, q, zero)], axis=0)
        s_ref[slot, :, pl.ds(0, kv_len)] = lax.dot_general(
            qs, k_ref[0, pl.ds(0, kv_len), :], nt_dims, preferred_element_type=F32)

        for r in range(2 * tq // rc):
            rows = pl.ds(r * rc, rc)
            keep = ((r * rc) % tq + row_i) >= col_i
            sd = jnp.where(keep, s_ref[slot, rows, pl.ds(off, tq)], -jnp.inf)
            m = jnp.max(sd, axis=-1, keepdims=True)
            if off > 0:
                so = s_ref[slot, rows, pl.ds(0, off)]
                m = jnp.maximum(m, jnp.max(so, axis=-1, keepdims=True))
                p_ref[slot, rows, pl.ds(0, off)] = jnp.exp2(so - m).astype(BF16)
            p_ref[slot, rows, pl.ds(off, tq)] = jnp.exp2(sd - m).astype(BF16)

        ol = jnp.dot(p_ref[slot, :, pl.ds(0, kv_len)], vext_ref[pl.ds(0, kv_len), :], preferred_element_type=F32)
        on = ol[:, :V_DIM] / ol[:, V_DIM:]
        o = on[:tq] - lam * on[tq:]
        o = o * lax.rsqrt(jnp.mean(o * o, axis=-1, keepdims=True) + EPS) * subg_ref[...]
        o_ref[0, pl.ds(off, tq), :] = (o * (1.0 - LAM_INIT)).astype(BF16)


def _attention(z3, lq1, lk1, lq2, lk2, subln_g):
    b, seq, _ = z3.shape
    small = lambda n: pl.BlockSpec((1, n), lambda bi, h: (0, 0))
    blk = lambda off: pl.BlockSpec((1, seq, V_DIM), lambda bi, h: (bi, 0, off + h))
    return pl.pallas_call(
        functools.partial(_attn_kernel, seq=seq),
        name="attn",
        grid=(b, N_HEADS),
        in_specs=[small(HEAD_DIM)] * 4 + [small(V_DIM), blk(Z_Q * N_HEADS), blk(Z_K * N_HEADS), blk(Z_V * N_HEADS)],
        out_specs=pl.BlockSpec((1, seq, V_DIM), lambda bi, h: (bi, 0, h)),
        out_shape=jax.ShapeDtypeStruct((b, seq, N_HEADS * V_DIM), BF16),
        scratch_shapes=[pltpu.VMEM((2, 2 * ATT_TQ, seq), F32), pltpu.VMEM((2, 2 * ATT_TQ, seq), BF16),
                        pltpu.VMEM((seq, 2 * V_DIM), BF16)],
        compiler_params=pltpu.CompilerParams(
            dimension_semantics=("parallel", "parallel"),
            vmem_limit_bytes=_vmem_limit(2 * 2 * ATT_TQ * seq * (4 + 2) + seq * 2 * V_DIM * 2
                                         + 2 * 4 * seq * V_DIM * 2 + (16 << 20))),
    )(lq1, lk1, lq2, lk2, subln_g, z3, z3, z3)


def _merge_kernel(x_ref, o_ref, u_ref, sv_ref, ga_ref, gs_ref, wsp_ref, bsp_ref, wa_ref, ws_ref, wo_ref,
                  g2_ref, h_ref, c_ref, sgu_ref):
    t_idx = lax.broadcasted_iota(jnp.int32, (CHUNK, CHUNK), 0)
    p_idx = lax.broadcasted_iota(jnp.int32, (CHUNK, CHUNK), 1)
    for g in range(N_GROUPS):
        cols = slice(g * GROUP_DIM, (g + 1) * GROUP_DIM)
        w = jnp.where(t_idx >= p_idx, wsp_ref[g], 0.0).astype(BF16)
        for c in range(MRG_TM // CHUNK):
            rows = pl.ds(c * CHUNK, CHUNK)
            s = jnp.dot(w, sv_ref[rows, cols], preferred_element_type=F32) + bsp_ref[:, cols]
            sgu_ref[rows, cols] = (u_ref[rows, cols].astype(F32) * s).astype(BF16)
    att_b = jnp.dot(o_ref[...], wa_ref[...], preferred_element_type=F32)
    sgu_b = jnp.dot(sgu_ref[...], ws_ref[...], preferred_element_type=F32)
    mix = (ga_ref[...].astype(F32) * att_b + gs_ref[...].astype(F32) * sgu_b).astype(BF16)
    h = x_ref[...] + jnp.dot(mix, wo_ref[...], preferred_element_type=F32)
    h_ref[...] = h
    ms = jnp.mean(h * h, axis=-1, keepdims=True)
    c_ref[...] = (h * lax.rsqrt(ms + EPS) * g2_ref[...]).astype(BF16)


def _merge(x2, o2, z2, sgu_w, bs_full, w_att_out, w_sgu_out, w_out, norm2_g):
    m = x2.shape[0]
    row = lambda i: (i, 0)
    const2 = lambda i: (0, 0)
    resident = functools.partial(pl.BlockSpec, pipeline_mode=pl.Buffered(1))
    zsec = lambda first, n: pl.BlockSpec((MRG_TM, n * SEC), lambda i: (i, first // n))
    vmem = ((SEC * D_MODEL * 2 + D_MODEL * D_MODEL) * 2
            + 2 * MRG_TM * (D_MODEL * 4 + 3 * SEC * 2 + 2 * D_MODEL * 2 + D_MODEL * 4 + D_MODEL * 2)
            + 6 * MRG_TM * D_MODEL * 4 + (6 << 20))
    return pl.pallas_call(
        _merge_kernel,
        name="merge",
        grid=(m // MRG_TM,),
        in_specs=[
            pl.BlockSpec((MRG_TM, D_MODEL), row),
            pl.BlockSpec((MRG_TM, SEC), row),
            zsec(Z_U, 1), zsec(Z_SV, 1), zsec(Z_GATT, 2), zsec(Z_GSGU, 2),
            resident((N_GROUPS, CHUNK, CHUNK), lambda i: (0, 0, 0)),
            resident((CHUNK, SEC), const2),
            resident((SEC, D_MODEL), const2),
            resident((SEC, D_MODEL), const2),
            resident((D_MODEL, D_MODEL), const2),
            pl.BlockSpec((1, D_MODEL), const2),
        ],
        out_specs=[pl.BlockSpec((MRG_TM, D_MODEL), row), pl.BlockSpec((MRG_TM, D_MODEL), row)],
        out_shape=[jax.ShapeDtypeStruct((m, D_MODEL), F32), jax.ShapeDtypeStruct((m, D_MODEL), BF16)],
        scratch_shapes=[pltpu.VMEM((MRG_TM, SEC), BF16)],
        compiler_params=pltpu.CompilerParams(
            dimension_semantics=("parallel",), vmem_limit_bytes=_vmem_limit(vmem)),
    )(x2, o2, z2, z2, z2, z2, sgu_w, bs_full, w_att_out, w_sgu_out, w_out, norm2_g)


def _ffn_kernel(c_ref, halo_ref, h_hbm, wg_ref, wv_ref, cwg_ref, cwv_ref, cbg_ref, cbv_ref, wd_ref,
                o_ref, cext_ref, ug_ref, uv_ref, act_ref, h_sem, *, blocks_per_seq):
    i = pl.program_id(0)
    j = pl.program_id(1)

    @pl.when(j == 0)
    def _():
        h_copy = pltpu.make_async_copy(h_hbm.at[pl.ds(i * FFN_TM, FFN_TM), :], o_ref, h_sem)
        h_copy.start()
        halo = halo_ref[...]
        cext_ref[pl.ds(0, FFN_HALO), :] = jnp.where(i % blocks_per_seq == 0, jnp.zeros_like(halo), halo)
        cext_ref[pl.ds(FFN_HALO, FFN_TM), :] = c_ref[...]
        h_copy.wait()

    def conv(u_ref, w_ref, b_ref, row0, cols):
        acc = b_ref[:, cols] + w_ref[pl.ds(CONV_WIDTH - 1, 1), cols] * u_ref[pl.ds(FFN_HALO + row0, FFN_RC), cols]
        for tap in range(CONV_WIDTH - 1):
            shift = CONV_WIDTH - 1 - tap
            acc = acc + w_ref[pl.ds(tap, 1), cols] * u_ref[pl.ds(FFN_HALO + row0 - shift, FFN_RC), cols]
        return acc

    for sub in range(FFN_TF // FFN_SUB):
        cols = slice(sub * FFN_SUB, (sub + 1) * FFN_SUB)
        ug_ref[:, cols] = jnp.dot(cext_ref[...], wg_ref[:, cols], preferred_element_type=F32)
        uv_ref[:, cols] = jnp.dot(cext_ref[...], wv_ref[:, cols], preferred_element_type=F32)
        for r in range(FFN_TM // FFN_RC):
            row0 = r * FFN_RC
            gate = _gelu(conv(ug_ref, cwg_ref, cbg_ref, row0, cols))
            act_ref[pl.ds(row0, FFN_RC), cols] = (gate * conv(uv_ref, cwv_ref, cbv_ref, row0, cols)).astype(BF16)

    for r in range(FFN_TM // FFN_RC):
        rows = pl.ds(r * FFN_RC, FFN_RC)
        o_ref[rows, :] += jnp.dot(act_ref[rows, :], wd_ref[...], preferred_element_type=F32)


def _ffn(c2, h2, w_up, conv_w, conv_b, w_down, seq):
    m = c2.shape[0]
    nf = D_FF // FFN_TF
    blocks_per_seq = seq // FFN_TM
    halo_blocks = FFN_TM // FFN_HALO
    ext = FFN_TM + FFN_HALO
    row = lambda i, j: (i, 0)
    gate_col = lambda i, j: (0, j)
    val_col = lambda i, j: (0, nf + j)
    vmem = (2 * FFN_TM * D_MODEL * 2 + ext * D_MODEL * 2 + 2 * FFN_TM * D_MODEL * 4
            + 4 * D_MODEL * FFN_TF * 2 + 2 * FFN_TF * D_MODEL * 2
            + 2 * ext * FFN_TF * 4 + FFN_TM * FFN_TF * 2
            + 4 * ext * FFN_SUB * 4 + 6 * FFN_RC * D_MODEL * 4 + (2 << 20))
    return pl.pallas_call(
        functools.partial(_ffn_kernel, blocks_per_seq=blocks_per_seq),
        name="ffn",
        grid=(m // FFN_TM, nf),
        in_specs=[
            pl.BlockSpec((FFN_TM, D_MODEL), row),
            pl.BlockSpec((FFN_HALO, D_MODEL), lambda i, j: (jnp.maximum(i * halo_blocks - 1, 0), 0)),
            pl.BlockSpec(memory_space=pl.ANY),
            pl.BlockSpec((D_MODEL, FFN_TF), gate_col),
            pl.BlockSpec((D_MODEL, FFN_TF), val_col),
            pl.BlockSpec((CONV_WIDTH, FFN_TF), gate_col),
            pl.BlockSpec((CONV_WIDTH, FFN_TF), val_col),
            pl.BlockSpec((1, FFN_TF), gate_col),
            pl.BlockSpec((1, FFN_TF), val_col),
            pl.BlockSpec((FFN_TF, D_MODEL), lambda i, j: (j, 0)),
        ],
        out_specs=pl.BlockSpec((FFN_TM, D_MODEL), row),
        out_shape=jax.ShapeDtypeStruct((m, D_MODEL), F32),
        scratch_shapes=[
            pltpu.VMEM((ext, D_MODEL), BF16),
            pltpu.VMEM((ext, FFN_TF), F32),
            pltpu.VMEM((ext, FFN_TF), F32),
            pltpu.VMEM((FFN_TM, FFN_TF), BF16),
            pltpu.SemaphoreType.DMA(()),
        ],
        compiler_params=pltpu.CompilerParams(
            dimension_semantics=("parallel", "arbitrary"), vmem_limit_bytes=_vmem_limit(vmem)),
    )(c2, c2, h2, w_up, w_up, conv_w, conv_w, conv_b, conv_b, w_down)


def _rope_tables(seq):
    inv = jnp.exp(-math.log(ROPE_THETA) * jnp.arange(0, HEAD_DIM, 2, dtype=F32) / HEAD_DIM)
    ang = jnp.arange(seq, dtype=F32)[:, None] * inv[None, :]
    cos, sin = jnp.cos(ang), jnp.sin(ang)
    return jnp.tile(cos, (1, 4)), jnp.tile(jnp.concatenate([-sin, sin], axis=1), (1, 2))


def kernel(x, norm1_g, w_in, b_gate, q_norm_g, k_norm_g, lambda_q1, lambda_k1, lambda_q2, lambda_k2, subln_g, sgu_norm_g, sgu_norm_b, sgu_w, sgu_b, w_att_out, w_sgu_out, w_out, norm2_g, w_up, conv_w, conv_b, w_down):
    bsz, seq, d = x.shape
    assert d == D_MODEL and w_in.shape[0] == 1
    assert seq % IN_TM == 0 and seq % FFN_TM == 0 and seq % ATT_TQ == 0 and seq % MRG_TM == 0
    m = bsz * seq
    x2 = x.reshape(m, d)
    cos_t, sin_t = _rope_tables(seq)
    seg = (jnp.arange(256)[:, None] // HEAD_DIM == jnp.arange(256)[None, :] // HEAD_DIM).astype(BF16)
    tile_heads = lambda g: jnp.tile(g, (1, SEC // HEAD_DIM))
    bs_full = jnp.repeat(sgu_b[0].T, GROUP_DIM, axis=1)

    z2 = _in_proj(x2, norm1_g, w_in[0].astype(BF16), b_gate, tile_heads(q_norm_g), tile_heads(k_norm_g),
                  cos_t, sin_t, seg, sgu_norm_g, sgu_norm_b, seq)
    o3 = _attention(z2.reshape(bsz, seq, N_SEC * SEC), lambda_q1, lambda_k1, lambda_q2, lambda_k2, subln_g)
    h2, c2 = _merge(x2, o3.reshape(m, SEC), z2, sgu_w[0], bs_full, w_att_out[0].astype(BF16),
                    w_sgu_out[0].astype(BF16), w_out[0].astype(BF16), norm2_g)
    out = _ffn(c2, h2, w_up[0].astype(BF16), conv_w[0], conv_b, w_down[0].astype(BF16), seq)
    return out.reshape(bsz, seq, d)
```

```python
import functools
import math

import jax
import jax.numpy as jnp
from jax import lax
from jax.experimental import pallas as pl
from jax.experimental.pallas import tpu as pltpu

F32 = jnp.float32
BF16 = jnp.bfloat16

D_MODEL = 2048
N_HEADS = 8
HEAD_DIM = 64
V_DIM = 2 * HEAD_DIM
SEC = 1024
N_SEC = 9
N_GATE_SEC = 4
Z_GATT, Z_GSGU, Z_Q, Z_K, Z_V, Z_U, Z_SV = 0, 2, 4, 5, 6, 7, 8
ROPE_THETA = 10000.0
CHUNK = 128
N_GROUPS = 8
GROUP_DIM = 128
D_FF = 5632
CONV_WIDTH = 3
EPS = 1e-6
LAM_INIT = 0.8 - 0.6 * math.exp(-0.3 * 0)
SQRT_HALF = math.sqrt(0.5)
LOG2_E = math.log2(math.e)

V7X_VMEM_BYTES = 64 * 1024 * 1024
LANES = 128
BF16_SUBLANES = 16

IN_TM = 1024
IN_RC = 256
ATT_TQ = 256
ATT_RC = 16
MRG_TM = 256
FFN_TM = 1024
FFN_TF = 512
FFN_SUB = 256
FFN_RC = 256
FFN_HALO = BF16_SUBLANES


def _gelu(x):
    return 0.5 * x * (1.0 + lax.erf(x * SQRT_HALF))


def _vmem_limit(nbytes):
    return int(min(V7X_VMEM_BYTES - (4 << 20), nbytes))


def _in_proj_kernel(x_ref, g1_ref, w_ref, bg_ref, qg_ref, kg_ref, cos_ref, sin_ref, seg_ref,
                    lng_ref, lnb_ref, o_ref, a_ref):
    j = pl.program_id(1)
    n_chunks = IN_TM // IN_RC

    @pl.when(j == 0)
    def _():
        for r in range(n_chunks):
            rows = pl.ds(r * IN_RC, IN_RC)
            xs = x_ref[rows, :]
            ms = jnp.mean(xs * xs, axis=-1, keepdims=True)
            a_ref[rows, :] = (xs * lax.rsqrt(ms + EPS) * g1_ref[...]).astype(BF16)

    def for_chunks(epilogue):
        for r in range(n_chunks):
            rows = pl.ds(r * IN_RC, IN_RC)
            z = jnp.dot(a_ref[rows, :], w_ref[...], preferred_element_type=F32)
            o_ref[rows, :] = epilogue(z, rows).astype(BF16)

    def qk_epilogue(gain_ref, scale):
        def ep(z, rows):
            sq = (z * z).astype(BF16)
            ss = jnp.concatenate(
                [jnp.dot(sq[:, c * 256:(c + 1) * 256], seg_ref[...], preferred_element_type=F32)
                 for c in range(SEC // 256)], axis=1)
            rinv = lax.rsqrt(ss * (1.0 / HEAD_DIM) + EPS)
            zg = z * gain_ref[...]
            lane = lax.broadcasted_iota(jnp.int32, zg.shape, 1)
            partner = jnp.where((lane & 32) == 0,
                                pltpu.roll(zg, SEC - 32, axis=1), pltpu.roll(zg, 32, axis=1))
            cos = jnp.concatenate([cos_ref[rows, :]] * N_HEADS, axis=1)
            sin = jnp.concatenate([sin_ref[rows, :]] * N_HEADS, axis=1)
            out = (zg * cos + partner * sin) * rinv
            return out * scale if scale != 1.0 else out
        return ep

    @pl.when(j == 0)
    def _():
        for_chunks(qk_epilogue(qg_ref, HEAD_DIM ** -0.5 * LOG2_E))

    @pl.when(j == 1)
    def _():
        for_chunks(qk_epilogue(kg_ref, 1.0))

    @pl.when(j == 2)
    def _():
        for_chunks(lambda z, rows: z)

    @pl.when(j == 3)
    def _():
        for_chunks(lambda z, rows: _gelu(z))

    @pl.when(j == 4)
    def _():
        def ep(z, rows):
            gl = _gelu(z)
            mu = jnp.mean(gl, axis=-1, keepdims=True)
            xc = gl - mu
            var = jnp.mean(xc * xc, axis=-1, keepdims=True)
            return xc * lax.rsqrt(var + EPS) * lng_ref[...] + lnb_ref[...]
        for_chunks(ep)

    @pl.when(j >= 5)
    def _():
        for_chunks(lambda z, rows: jax.nn.sigmoid(z + bg_ref[...]))


def _in_proj(x2, norm1_g, w_in, b_gate, qg_t, kg_t, cos_t, sin_t, seg, ln_g, ln_b, seq):
    m = x2.shape[0]
    blocks_per_seq = seq // IN_TM
    row = lambda i, j: (i, 0)
    const = lambda i, j: (0, 0)
    vmem = (2 * IN_TM * D_MODEL * 4 + 2 * D_MODEL * SEC * 2 + IN_TM * D_MODEL * 2
            + 2 * IN_TM * SEC * 2 + 12 * IN_RC * SEC * 4 + (4 << 20))
    return pl.pallas_call(
        _in_proj_kernel,
        name="in_proj",
        grid=(m // IN_TM, N_SEC),
        in_specs=[
            pl.BlockSpec((IN_TM, D_MODEL), row),
            pl.BlockSpec((1, D_MODEL), const),
            pl.BlockSpec((None, D_MODEL, SEC), lambda i, j: (j, 0, 0)),
            pl.BlockSpec((1, SEC), lambda i, j: (0, jnp.maximum(j - 5, 0))),
            pl.BlockSpec((1, SEC), const),
            pl.BlockSpec((1, SEC), const),
            pl.BlockSpec((IN_TM, LANES), lambda i, j: (i % blocks_per_seq, 0)),
            pl.BlockSpec((IN_TM, LANES), lambda i, j: (i % blocks_per_seq, 0)),
            pl.BlockSpec((256, 256), const),
            pl.BlockSpec((1, SEC), const),
            pl.BlockSpec((1, SEC), const),
        ],
        out_specs=pl.BlockSpec((IN_TM, SEC), lambda i, j: (i, (j + N_GATE_SEC) % N_SEC)),
        out_shape=jax.ShapeDtypeStruct((m, N_SEC * SEC), BF16),
        scratch_shapes=[pltpu.VMEM((IN_TM, D_MODEL), BF16)],
        compiler_params=pltpu.CompilerParams(
            dimension_semantics=("parallel", "arbitrary"), vmem_limit_bytes=_vmem_limit(vmem)),
    )(x2, norm1_g, w_in, b_gate, qg_t, kg_t, cos_t, sin_t, seg, ln_g, ln_b)


def _attn_kernel(lq1_ref, lk1_ref, lq2_ref, lk2_ref, subg_ref, q_ref, k_ref, v_ref, wu_ref, wd_ref,
                 o_ref, wu_bf_ref, wd_bf_ref, s_ref, p_ref, vext_ref, *, seq):
    wu_bf_ref[...] = wu_ref[...].astype(BF16)
    wd_bf_ref[...] = wd_ref[...].astype(BF16)

    tq = ATT_TQ
    lam = (jnp.exp(jnp.sum(lq1_ref[...] * lk1_ref[...], axis=-1, keepdims=True))
           - jnp.exp(jnp.sum(lq2_ref[...] * lk2_ref[...], axis=-1, keepdims=True)) + LAM_INIT)
    rc = ATT_RC
    lane = lax.broadcasted_iota(jnp.int32, (tq, V_DIM), 1)
    row_i = lax.broadcasted_iota(jnp.int32, (rc, tq), 0)
    col_i = lax.broadcasted_iota(jnp.int32, (rc, tq), 1)
    nt_dims = (((1,), (1,)), ((), ()))

    vext_ref[:, pl.ds(0, V_DIM)] = v_ref[0]
    vext_ref[:, pl.ds(V_DIM, V_DIM)] = jnp.ones((seq, V_DIM), BF16)

    for qi in range(seq // tq):
        slot = qi % 2
        off = qi * tq
        kv_len = off + tq
        q = q_ref[0, pl.ds(off, tq), :]
        zero = jnp.zeros_like(q)
        qs = jnp.concatenate([jnp.where(lane < HEAD_DIM, q, zero), jnp.where(lane >= HEAD_DIM, q, zero)], axis=0)
        s_ref[slot, :, pl.ds(0, kv_len)] = lax.dot_general(
            qs, k_ref[0, pl.ds(0, kv_len), :], nt_dims, preferred_element_type=F32)

        for r in range(2 * tq // rc):
            rows = pl.ds(r * rc, rc)
            keep = ((r * rc) % tq + row_i) >= col_i
            sd = jnp.where(keep, s_ref[slot, rows, pl.ds(off, tq)], -jnp.inf)
            m = jnp.max(sd, axis=-1, keepdims=True)
            if off > 0:
                so = s_ref[slot, rows, pl.ds(0, off)]
                m = jnp.maximum(m, jnp.max(so, axis=-1, keepdims=True))
                p_ref[slot, rows, pl.ds(0, off)] = jnp.exp2(so - m).astype(BF16)
            p_ref[slot, rows, pl.ds(off, tq)] = jnp.exp2(sd - m).astype(BF16)

        ol = jnp.dot(p_ref[slot, :, pl.ds(0, kv_len)], vext_ref[pl.ds(0, kv_len), :], preferred_element_type=F32)
        on = ol[:, :V_DIM] / ol[:, V_DIM:]
        o = on[:tq] - lam * on[tq:]
        o = o * lax.rsqrt(jnp.mean(o * o, axis=-1, keepdims=True) + EPS) * subg_ref[...]
        o_ref[0, pl.ds(off, tq), :] = (o * (1.0 - LAM_INIT)).astype(BF16)


def _attention(z3, lq1, lk1, lq2, lk2, subln_g, w_up, w_down):
    b, seq, _ = z3.shape
    n_up_tiles = 2 * D_FF // FFN_TF
    n_cast = 2 * n_up_tiles
    dn_rows = D_FF // n_cast
    assert b * N_HEADS >= n_cast and D_FF % n_cast == 0 and dn_rows % BF16_SUBLANES == 0
    cast_id = lambda bi, h: jnp.minimum(bi * N_HEADS + h, n_cast - 1)
    small = lambda n: pl.BlockSpec((1, n), lambda bi, h: (0, 0))
    blk = lambda off: pl.BlockSpec((1, seq, V_DIM), lambda bi, h: (bi, 0, off + h))
    return pl.pallas_call(
        functools.partial(_attn_kernel, seq=seq),
        name="attn",
        grid=(b, N_HEADS),
        in_specs=[small(HEAD_DIM)] * 4 + [small(V_DIM), blk(Z_Q * N_HEADS), blk(Z_K * N_HEADS), blk(Z_V * N_HEADS),
                  pl.BlockSpec((D_MODEL // 2, FFN_TF), lambda bi, h: (cast_id(bi, h) % 2, cast_id(bi, h) // 2)),
                  pl.BlockSpec((dn_rows, D_MODEL), lambda bi, h: (cast_id(bi, h), 0))],
        out_specs=[pl.BlockSpec((1, seq, V_DIM), lambda bi, h: (bi, 0, h)),
                   pl.BlockSpec((None, D_MODEL // 2, FFN_TF),
                                lambda bi, h: (cast_id(bi, h) // 2, cast_id(bi, h) % 2, 0)),
                   pl.BlockSpec((dn_rows, D_MODEL), lambda bi, h: (cast_id(bi, h), 0))],
        out_shape=[jax.ShapeDtypeStruct((b, seq, N_HEADS * V_DIM), BF16),
                   jax.ShapeDtypeStruct((n_up_tiles, D_MODEL, FFN_TF), BF16),
                   jax.ShapeDtypeStruct((D_FF, D_MODEL), BF16)],
        scratch_shapes=[pltpu.VMEM((2, 2 * ATT_TQ, seq), F32), pltpu.VMEM((2, 2 * ATT_TQ, seq), BF16),
                        pltpu.VMEM((seq, 2 * V_DIM), BF16)],
        compiler_params=pltpu.CompilerParams(
            dimension_semantics=("arbitrary", "arbitrary"),
            vmem_limit_bytes=_vmem_limit(2 * 2 * ATT_TQ * seq * (4 + 2) + seq * 2 * V_DIM * 2
                                         + 2 * 4 * seq * V_DIM * 2
                                         + 2 * (D_MODEL // 2 * FFN_TF + dn_rows * D_MODEL) * (4 + 2) + (16 << 20))),
    )(lq1, lk1, lq2, lk2, subln_g, z3, z3, z3, w_up, w_down)


def _merge_kernel(x_ref, o_ref, u_ref, sv_ref, ga_ref, gs_ref, wsp_ref, bsp_ref, wa_ref, ws_ref, wo_ref,
                  g2_ref, h_ref, c_ref, sgu_ref):
    t_idx = lax.broadcasted_iota(jnp.int32, (CHUNK, CHUNK), 0)
    p_idx = lax.broadcasted_iota(jnp.int32, (CHUNK, CHUNK), 1)
    for g in range(N_GROUPS):
        cols = slice(g * GROUP_DIM, (g + 1) * GROUP_DIM)
        w = jnp.where(t_idx >= p_idx, wsp_ref[g], 0.0).astype(BF16)
        for c in range(MRG_TM // CHUNK):
            rows = pl.ds(c * CHUNK, CHUNK)
            s = jnp.dot(w, sv_ref[rows, cols], preferred_element_type=F32) + bsp_ref[:, cols]
            sgu_ref[rows, cols] = (u_ref[rows, cols].astype(F32) * s).astype(BF16)
    att_b = jnp.dot(o_ref[...], wa_ref[...], preferred_element_type=F32)
    sgu_b = jnp.dot(sgu_ref[...], ws_ref[...], preferred_element_type=F32)
    mix = (ga_ref[...].astype(F32) * att_b + gs_ref[...].astype(F32) * sgu_b).astype(BF16)
    h = x_ref[...] + jnp.dot(mix, wo_ref[...], preferred_element_type=F32)
    h_ref[...] = h
    ms = jnp.mean(h * h, axis=-1, keepdims=True)
    c_ref[...] = (h * lax.rsqrt(ms + EPS) * g2_ref[...]).astype(BF16)


def _merge(x2, o2, z2, sgu_w, bs_full, w_att_out, w_sgu_out, w_out, norm2_g):
    m = x2.shape[0]
    row = lambda i: (i, 0)
    const2 = lambda i: (0, 0)
    resident = functools.partial(pl.BlockSpec, pipeline_mode=pl.Buffered(1))
    zsec = lambda first, n: pl.BlockSpec((MRG_TM, n * SEC), lambda i: (i, first // n))
    vmem = ((SEC * D_MODEL * 2 + D_MODEL * D_MODEL) * 2
            + 2 * MRG_TM * (D_MODEL * 4 + 3 * SEC * 2 + 2 * D_MODEL * 2 + D_MODEL * 4 + D_MODEL * 2)
            + 6 * MRG_TM * D_MODEL * 4 + (6 << 20))
    return pl.pallas_call(
        _merge_kernel,
        name="merge",
        grid=(m // MRG_TM,),
        in_specs=[
            pl.BlockSpec((MRG_TM, D_MODEL), row),
            pl.BlockSpec((MRG_TM, SEC), row),
            zsec(Z_U, 1), zsec(Z_SV, 1), zsec(Z_GATT, 2), zsec(Z_GSGU, 2),
            resident((N_GROUPS, CHUNK, CHUNK), lambda i: (0, 0, 0)),
            resident((CHUNK, SEC), const2),
            resident((SEC, D_MODEL), const2),
            resident((SEC, D_MODEL), const2),
            resident((D_MODEL, D_MODEL), const2),
            pl.BlockSpec((1, D_MODEL), const2),
        ],
        out_specs=[pl.BlockSpec((MRG_TM, D_MODEL), row), pl.BlockSpec((MRG_TM, D_MODEL), row)],
        out_shape=[jax.ShapeDtypeStruct((m, D_MODEL), F32), jax.ShapeDtypeStruct((m, D_MODEL), BF16)],
        scratch_shapes=[pltpu.VMEM((MRG_TM, SEC), BF16)],
        compiler_params=pltpu.CompilerParams(
            dimension_semantics=("parallel",), vmem_limit_bytes=_vmem_limit(vmem)),
    )(x2, o2, z2, z2, z2, z2, sgu_w, bs_full, w_att_out, w_sgu_out, w_out, norm2_g)


def _ffn_kernel(c_ref, halo_ref, h_hbm, wg_ref, wv_ref, cwg_ref, cwv_ref, cbg_ref, cbv_ref, wd_ref,
                o_ref, cext_ref, ug0_ref, ug1_ref, uv0_ref, uv1_ref, act_ref, h_sem, *, blocks_per_seq):
    ug_refs, uv_refs = (ug0_ref, ug1_ref), (uv0_ref, uv1_ref)
    i = pl.program_id(0)
    j = pl.program_id(1)

    @pl.when(j == 0)
    def _():
        h_copy = pltpu.make_async_copy(h_hbm.at[pl.ds(i * FFN_TM, FFN_TM), :], o_ref, h_sem)
        h_copy.start()
        halo = halo_ref[...]
        cext_ref[pl.ds(0, FFN_HALO), :] = jnp.where(i % blocks_per_seq == 0, jnp.zeros_like(halo), halo)
        cext_ref[pl.ds(FFN_HALO, FFN_TM), :] = c_ref[...]
        h_copy.wait()

    def conv(u_ref, w_ref, b_ref, row0, cols):
        acc = b_ref[:, cols] + w_ref[pl.ds(CONV_WIDTH - 1, 1), cols] * u_ref[pl.ds(FFN_HALO + row0, FFN_RC), :]
        for tap in range(CONV_WIDTH - 1):
            shift = CONV_WIDTH - 1 - tap
            acc = acc + w_ref[pl.ds(tap, 1), cols] * u_ref[pl.ds(FFN_HALO + row0 - shift, FFN_RC), :]
        return acc

    half = FFN_TM // 2
    chunks_per_half = half // FFN_RC
    for sub in range(FFN_TF // FFN_SUB):
        cols = slice(sub * FFN_SUB, (sub + 1) * FFN_SUB)
        ug_ref, uv_ref = ug_refs[sub], uv_refs[sub]
        for seg in range(2):
            ext_rows = pl.ds(0, FFN_HALO + half) if seg == 0 else pl.ds(FFN_HALO + half, half)
            ug_ref[ext_rows, :] = jnp.dot(cext_ref[ext_rows, :], wg_ref[:, cols], preferred_element_type=F32)
            uv_ref[ext_rows, :] = jnp.dot(cext_ref[ext_rows, :], wv_ref[:, cols], preferred_element_type=F32)
            for r in range(seg * chunks_per_half, (seg + 1) * chunks_per_half):
                row0 = r * FFN_RC
                gate = _gelu(conv(ug_ref, cwg_ref, cbg_ref, row0, cols))
                act_ref[pl.ds(row0, FFN_RC), cols] = (
                    gate * conv(uv_ref, cwv_ref, cbv_ref, row0, cols)).astype(BF16)

    for r in range(FFN_TM // FFN_RC):
        rows = pl.ds(r * FFN_RC, FFN_RC)
        o_ref[rows, :] += jnp.dot(act_ref[rows, :], wd_ref[...], preferred_element_type=F32)


def _ffn(c2, h2, w_up_tiles, conv_w, conv_b, w_down, seq):
    m = c2.shape[0]
    nf = D_FF // FFN_TF
    blocks_per_seq = seq // FFN_TM
    halo_blocks = FFN_TM // FFN_HALO
    ext = FFN_TM + FFN_HALO
    row = lambda i, j: (i, 0)
    gate_col = lambda i, j: (0, j)
    val_col = lambda i, j: (0, nf + j)
    assert FFN_TF == 2 * FFN_SUB and w_up_tiles.shape == (2 * nf, D_MODEL, FFN_TF)
    vmem = (2 * FFN_TM * D_MODEL * 2 + ext * D_MODEL * 2 + 2 * FFN_TM * D_MODEL * 4
            + 4 * D_MODEL * FFN_TF * 2 + 2 * FFN_TF * D_MODEL * 2
            + 2 * ext * FFN_TF * 4 + FFN_TM * FFN_TF * 2
            + 4 * ext * FFN_SUB * 4 + 6 * FFN_RC * D_MODEL * 4 + (2 << 20))
    return pl.pallas_call(
        functools.partial(_ffn_kernel, blocks_per_seq=blocks_per_seq),
        name="ffn",
        grid=(m // FFN_TM, nf),
        in_specs=[
            pl.BlockSpec((FFN_TM, D_MODEL), row),
            pl.BlockSpec((FFN_HALO, D_MODEL), lambda i, j: (jnp.maximum(i * halo_blocks - 1, 0), 0)),
            pl.BlockSpec(memory_space=pl.ANY),
            pl.BlockSpec((None, D_MODEL, FFN_TF), lambda i, j: (j, 0, 0)),
            pl.BlockSpec((None, D_MODEL, FFN_TF), lambda i, j: (nf + j, 0, 0)),
            pl.BlockSpec((CONV_WIDTH, FFN_TF), gate_col),
            pl.BlockSpec((CONV_WIDTH, FFN_TF), val_col),
            pl.BlockSpec((1, FFN_TF), gate_col),
            pl.BlockSpec((1, FFN_TF), val_col),
            pl.BlockSpec((FFN_TF, D_MODEL), lambda i, j: (j, 0)),
        ],
        out_specs=pl.BlockSpec((FFN_TM, D_MODEL), row),
        out_shape=jax.ShapeDtypeStruct((m, D_MODEL), F32),
        scratch_shapes=[
            pltpu.VMEM((ext, D_MODEL), BF16),
            *[pltpu.VMEM((ext, FFN_SUB), F32)] * (2 * (FFN_TF // FFN_SUB)),
            pltpu.VMEM((FFN_TM, FFN_TF), BF16),
            pltpu.SemaphoreType.DMA(()),
        ],
        compiler_params=pltpu.CompilerParams(
            dimension_semantics=("parallel", "arbitrary"), vmem_limit_bytes=_vmem_limit(vmem)),
    )(c2, c2, h2, w_up_tiles, w_up_tiles, conv_w, conv_w, conv_b, conv_b, w_down)


def _rope_tables(seq):
    inv = jnp.exp(-math.log(ROPE_THETA) * jnp.arange(0, HEAD_DIM, 2, dtype=F32) / HEAD_DIM)
    ang = jnp.arange(seq, dtype=F32)[:, None] * inv[None, :]
    cos, sin = jnp.cos(ang), jnp.sin(ang)
    return jnp.tile(cos, (1, 4)), jnp.tile(jnp.concatenate([-sin, sin], axis=1), (1, 2))


def kernel(x, norm1_g, w_in, b_gate, q_norm_g, k_norm_g, lambda_q1, lambda_k1, lambda_q2, lambda_k2, subln_g, sgu_norm_g, sgu_norm_b, sgu_w, sgu_b, w_att_out, w_sgu_out, w_out, norm2_g, w_up, conv_w, conv_b, w_down):
    bsz, seq, d = x.shape
    assert d == D_MODEL and w_in.shape[0] == 1
    assert seq % IN_TM == 0 and seq % FFN_TM == 0 and seq % ATT_TQ == 0 and seq % MRG_TM == 0
    m = bsz * seq
    x2 = x.reshape(m, d)
    cos_t, sin_t = _rope_tables(seq)
    seg = (jnp.arange(256)[:, None] // HEAD_DIM == jnp.arange(256)[None, :] // HEAD_DIM).astype(BF16)
    tile_heads = lambda g: jnp.tile(g, (1, SEC // HEAD_DIM))
    bs_full = jnp.repeat(sgu_b[0].T, GROUP_DIM, axis=1)

    w_in_sec = w_in[0].reshape(D_MODEL, N_SEC, SEC).transpose(1, 0, 2).astype(BF16)

    z2 = _in_proj(x2, norm1_g, w_in_sec, b_gate, tile_heads(q_norm_g), tile_heads(k_norm_g),
                  cos_t, sin_t, seg, sgu_norm_g, sgu_norm_b, seq)
    o3, w_up_tiles, w_down_bf = _attention(z2.reshape(bsz, seq, N_SEC * SEC), lambda_q1, lambda_k1, lambda_q2,
                                           lambda_k2, subln_g, w_up[0], w_down[0])
    h2, c2 = _merge(x2, o3.reshape(m, SEC), z2, sgu_w[0], bs_full, w_att_out[0].astype(BF16),
                    w_sgu_out[0].astype(BF16), w_out[0].astype(BF16), norm2_g)
    out = _ffn(c2, h2, w_up_tiles, conv_w[0], conv_b, w_down_bf, seq)
    return out.reshape(bsz, seq, d)
```

```python
import functools
import math

import jax
import jax.numpy as jnp
from jax import lax
from jax.experimental import pallas as pl
from jax.experimental.pallas import tpu as pltpu

F32 = jnp.float32
BF16 = jnp.bfloat16

D_MODEL = 2048
N_HEADS = 8
HEAD_DIM = 64
V_DIM = 2 * HEAD_DIM
SEC = 1024
N_SEC = 9
N_GATE_SEC = 4
Z_GATT, Z_GSGU, Z_Q, Z_K, Z_V, Z_U, Z_SV = 0, 2, 4, 5, 6, 7, 8
ROPE_THETA = 10000.0
CHUNK = 128
N_GROUPS = 8
GROUP_DIM = 128
D_FF = 5632
CONV_WIDTH = 3
EPS = 1e-6
LAM_INIT = 0.8 - 0.6 * math.exp(-0.3 * 0)
SQRT_HALF = math.sqrt(0.5)
LOG2_E = math.log2(math.e)

V7X_VMEM_BYTES = 64 * 1024 * 1024
LANES = 128
BF16_SUBLANES = 16

IN_TM = 1024
IN_RC = 256
CAST_ROWS = 64
ATT_TQ = 256
ATT_RC = 16
MRG_TM = 256
FFN_TM = 1024
FFN_TF = 512
FFN_SUB = 256
FFN_RC = 256
FFN_HALO = BF16_SUBLANES


def _gelu(x):
    return 0.5 * x * (1.0 + lax.erf(x * SQRT_HALF))


def _vmem_limit(nbytes):
    return int(min(V7X_VMEM_BYTES - (4 << 20), nbytes))


def _in_proj_kernel(x_ref, g1_ref, w_ref, bg_ref, qg_ref, kg_ref, cos_ref, sin_ref, seg_ref,
                    lng_ref, lnb_ref, wa_ref, ws_ref, wo_ref, o_ref, wa_bf_ref, ws_bf_ref, wo_bf_ref, a_ref):
    wa_bf_ref[...] = wa_ref[...].astype(BF16)
    ws_bf_ref[...] = ws_ref[...].astype(BF16)
    wo_bf_ref[...] = wo_ref[...].astype(BF16)

    j = pl.program_id(1)
    n_chunks = IN_TM // IN_RC

    @pl.when(j == 0)
    def _():
        for r in range(n_chunks):
            rows = pl.ds(r * IN_RC, IN_RC)
            xs = x_ref[rows, :]
            ms = jnp.mean(xs * xs, axis=-1, keepdims=True)
            a_ref[rows, :] = (xs * lax.rsqrt(ms + EPS) * g1_ref[...]).astype(BF16)

    def for_chunks(epilogue):
        for r in range(n_chunks):
            rows = pl.ds(r * IN_RC, IN_RC)
            z = jnp.dot(a_ref[rows, :], w_ref[...], preferred_element_type=F32)
            o_ref[rows, :] = epilogue(z, rows).astype(BF16)

    def qk_epilogue(gain_ref, scale):
        def ep(z, rows):
            sq = (z * z).astype(BF16)
            ss = jnp.concatenate(
                [jnp.dot(sq[:, c * 256:(c + 1) * 256], seg_ref[...], preferred_element_type=F32)
                 for c in range(SEC // 256)], axis=1)
            rinv = lax.rsqrt(ss * (1.0 / HEAD_DIM) + EPS)
            zg = z * gain_ref[...]
            lane = lax.broadcasted_iota(jnp.int32, zg.shape, 1)
            partner = jnp.where((lane & 32) == 0,
                                pltpu.roll(zg, SEC - 32, axis=1), pltpu.roll(zg, 32, axis=1))
            cos = jnp.concatenate([cos_ref[rows, :]] * N_HEADS, axis=1)
            sin = jnp.concatenate([sin_ref[rows, :]] * N_HEADS, axis=1)
            out = (zg * cos + partner * sin) * rinv
            return out * scale if scale != 1.0 else out
        return ep

    @pl.when(j == 0)
    def _():
        for_chunks(qk_epilogue(qg_ref, HEAD_DIM ** -0.5 * LOG2_E))

    @pl.when(j == 1)
    def _():
        for_chunks(qk_epilogue(kg_ref, 1.0))

    @pl.when(j == 2)
    def _():
        for_chunks(lambda z, rows: z)

    @pl.when(j == 3)
    def _():
        for_chunks(lambda z, rows: _gelu(z))

    @pl.when(j == 4)
    def _():
        def ep(z, rows):
            gl = _gelu(z)
            mu = jnp.mean(gl, axis=-1, keepdims=True)
            xc = gl - mu
            var = jnp.mean(xc * xc, axis=-1, keepdims=True)
            return xc * lax.rsqrt(var + EPS) * lng_ref[...] + lnb_ref[...]
        for_chunks(ep)

    @pl.when(j >= 5)
    def _():
        for_chunks(lambda z, rows: jax.nn.sigmoid(z + bg_ref[...]))


def _in_proj(x2, norm1_g, w_in, b_gate, qg_t, kg_t, cos_t, sin_t, seg, ln_g, ln_b, seq, merge_weights):
    m = x2.shape[0]
    blocks_per_seq = seq // IN_TM
    row = lambda i, j: (i, 0)
    const = lambda i, j: (0, 0)
    cast_specs, first = [], 0
    for w in merge_weights:
        n_blocks = w.shape[0] // CAST_ROWS
        index = functools.partial(
            lambda i, j, first, n_blocks: (jnp.clip(i * N_SEC + j - first, 0, n_blocks - 1), 0),
            first=first, n_blocks=n_blocks)
        cast_specs.append(pl.BlockSpec((CAST_ROWS, w.shape[1]), index))
        first += n_blocks
    assert first <= (m // IN_TM) * N_SEC
    vmem = (2 * IN_TM * D_MODEL * 4 + 2 * D_MODEL * SEC * 2 + IN_TM * D_MODEL * 2
            + 2 * IN_TM * SEC * 2 + 12 * IN_RC * SEC * 4
            + 2 * len(merge_weights) * CAST_ROWS * D_MODEL * (4 + 2) + (4 << 20))
    return pl.pallas_call(
        _in_proj_kernel,
        name="in_proj",
        grid=(m // IN_TM, N_SEC),
        in_specs=[
            pl.BlockSpec((IN_TM, D_MODEL), row),
            pl.BlockSpec((1, D_MODEL), const),
            pl.BlockSpec((D_MODEL, SEC), lambda i, j: (0, j)),
            pl.BlockSpec((1, SEC), lambda i, j: (0, jnp.maximum(j - 5, 0))),
            pl.BlockSpec((1, SEC), const),
            pl.BlockSpec((1, SEC), const),
            pl.BlockSpec((IN_TM, LANES), lambda i, j: (i % blocks_per_seq, 0)),
            pl.BlockSpec((IN_TM, LANES), lambda i, j: (i % blocks_per_seq, 0)),
            pl.BlockSpec((256, 256), const),
            pl.BlockSpec((1, SEC), const),
            pl.BlockSpec((1, SEC), const),
            *cast_specs,
        ],
        out_specs=[pl.BlockSpec((IN_TM, SEC), lambda i, j: (i, (j + N_GATE_SEC) % N_SEC)), *cast_specs],
        out_shape=[jax.ShapeDtypeStruct((m, N_SEC * SEC), BF16),
                   *[jax.ShapeDtypeStruct(w.shape, BF16) for w in merge_weights]],
        scratch_shapes=[pltpu.VMEM((IN_TM, D_MODEL), BF16)],
        compiler_params=pltpu.CompilerParams(
            dimension_semantics=("arbitrary", "arbitrary"), vmem_limit_bytes=_vmem_limit(vmem)),
    )(x2, norm1_g, w_in, b_gate, qg_t, kg_t, cos_t, sin_t, seg, ln_g, ln_b, *merge_weights)


def _attn_kernel(lq1_ref, lk1_ref, lq2_ref, lk2_ref, subg_ref, q_ref, k_ref, v_ref, wu_ref, wd_ref,
                 o_ref, wu_bf_ref, wd_bf_ref, s_ref, p_ref, vext_ref, *, seq):
    wu_bf_ref[...] = wu_ref[...].astype(BF16)
    wd_bf_ref[...] = wd_ref[...].astype(BF16)

    tq = ATT_TQ
    lam = (jnp.exp(jnp.sum(lq1_ref[...] * lk1_ref[...], axis=-1, keepdims=True))
           - jnp.exp(jnp.sum(lq2_ref[...] * lk2_ref[...], axis=-1, keepdims=True)) + LAM_INIT)
    rc = ATT_RC
    lane = lax.broadcasted_iota(jnp.int32, (tq, V_DIM), 1)
    row_i = lax.broadcasted_iota(jnp.int32, (rc, tq), 0)
    col_i = lax.broadcasted_iota(jnp.int32, (rc, tq), 1)
    nt_dims = (((1,), (1,)), ((), ()))

    vext_ref[:, pl.ds(0, V_DIM)] = v_ref[0]
    vext_ref[:, pl.ds(V_DIM, V_DIM)] = jnp.ones((seq, V_DIM), BF16)

    def scores(qi):
        off = qi * tq
        q = q_ref[0, pl.ds(off, tq), :]
        zero = jnp.zeros_like(q)
        qs = jnp.concatenate([jnp.where(lane < HEAD_DIM, q, zero), jnp.where(lane >= HEAD_DIM, q, zero)], axis=0)
        s_ref[qi % 2, :, pl.ds(0, off + tq)] = lax.dot_general(
            qs, k_ref[0, pl.ds(0, off + tq), :], nt_dims, preferred_element_type=F32)

    n_q = seq // tq
    scores(0)
    for qi in range(n_q):
        slot = qi % 2
        off = qi * tq
        kv_len = off + tq
        if qi + 1 < n_q:
            scores(qi + 1)

        for r in range(2 * tq // rc):
            rows = pl.ds(r * rc, rc)
            keep = ((r * rc) % tq + row_i) >= col_i
            sd = jnp.where(keep, s_ref[slot, rows, pl.ds(off, tq)], -jnp.inf)
            m = jnp.max(sd, axis=-1, keepdims=True)
            if off > 0:
                so = s_ref[slot, rows, pl.ds(0, off)]
                m = jnp.maximum(m, jnp.max(so, axis=-1, keepdims=True))
                p_ref[slot, rows, pl.ds(0, off)] = jnp.exp2(so - m).astype(BF16)
            p_ref[slot, rows, pl.ds(off, tq)] = jnp.exp2(sd - m).astype(BF16)

        ol = jnp.dot(p_ref[slot, :, pl.ds(0, kv_len)], vext_ref[pl.ds(0, kv_len), :], preferred_element_type=F32)
        on = ol[:, :V_DIM] / ol[:, V_DIM:]
        o = on[:tq] - lam * on[tq:]
        o = o * lax.rsqrt(jnp.mean(o * o, axis=-1, keepdims=True) + EPS) * subg_ref[...]
        o_ref[0, pl.ds(off, tq), :] = (o * (1.0 - LAM_INIT)).astype(BF16)


def _attention(z3, lq1, lk1, lq2, lk2, subln_g, w_up, w_down):
    b, seq, _ = z3.shape
    n_up_tiles = 2 * D_FF // FFN_TF
    n_cast = 2 * n_up_tiles
    dn_rows = D_FF // n_cast
    assert b * N_HEADS >= n_cast and D_FF % n_cast == 0 and dn_rows % BF16_SUBLANES == 0
    cast_id = lambda bi, h: jnp.minimum(bi * N_HEADS + h, n_cast - 1)
    small = lambda n: pl.BlockSpec((1, n), lambda bi, h: (0, 0))
    blk = lambda off: pl.BlockSpec((1, seq, V_DIM), lambda bi, h: (bi, 0, off + h))
    return pl.pallas_call(
        functools.partial(_attn_kernel, seq=seq),
        name="attn",
        grid=(b, N_HEADS),
        in_specs=[small(HEAD_DIM)] * 4 + [small(V_DIM), blk(Z_Q * N_HEADS), blk(Z_K * N_HEADS), blk(Z_V * N_HEADS),
                  pl.BlockSpec((D_MODEL // 2, FFN_TF), lambda bi, h: (cast_id(bi, h) % 2, cast_id(bi, h) // 2)),
                  pl.BlockSpec((dn_rows, D_MODEL), lambda bi, h: (cast_id(bi, h), 0))],
        out_specs=[pl.BlockSpec((1, seq, V_DIM), lambda bi, h: (bi, 0, h)),
                   pl.BlockSpec((D_MODEL // 2, FFN_TF), lambda bi, h: (cast_id(bi, h) % 2, cast_id(bi, h) // 2)),
                   pl.BlockSpec((dn_rows, D_MODEL), lambda bi, h: (cast_id(bi, h), 0))],
        out_shape=[jax.ShapeDtypeStruct((b, seq, N_HEADS * V_DIM), BF16),
                   jax.ShapeDtypeStruct((D_MODEL, 2 * D_FF), BF16),
                   jax.ShapeDtypeStruct((D_FF, D_MODEL), BF16)],
        scratch_shapes=[pltpu.VMEM((2, 2 * ATT_TQ, seq), F32), pltpu.VMEM((2, 2 * ATT_TQ, seq), BF16),
                        pltpu.VMEM((seq, 2 * V_DIM), BF16)],
        compiler_params=pltpu.CompilerParams(
            dimension_semantics=("arbitrary", "arbitrary"),
            vmem_limit_bytes=_vmem_limit(2 * 2 * ATT_TQ * seq * (4 + 2) + seq * 2 * V_DIM * 2
                                         + 2 * 4 * seq * V_DIM * 2
                                         + 2 * (D_MODEL // 2 * FFN_TF + dn_rows * D_MODEL) * (4 + 2) + (16 << 20))),
    )(lq1, lk1, lq2, lk2, subln_g, z3, z3, z3, w_up, w_down)


def _merge_kernel(x_ref, o_ref, u_ref, sv_ref, ga_ref, gs_ref, wsp_ref, bsp_ref, wa_ref, ws_ref, wo_ref,
                  g2_ref, h_ref, c_ref, sgu_ref):
    t_idx = lax.broadcasted_iota(jnp.int32, (CHUNK, CHUNK), 0)
    p_idx = lax.broadcasted_iota(jnp.int32, (CHUNK, CHUNK), 1)
    for g in range(N_GROUPS):
        cols = slice(g * GROUP_DIM, (g + 1) * GROUP_DIM)
        w = jnp.where(t_idx >= p_idx, wsp_ref[g], 0.0).astype(BF16)
        for c in range(MRG_TM // CHUNK):
            rows = pl.ds(c * CHUNK, CHUNK)
            s = jnp.dot(w, sv_ref[rows, cols], preferred_element_type=F32) + bsp_ref[:, cols]
            sgu_ref[rows, cols] = (u_ref[rows, cols].astype(F32) * s).astype(BF16)
    att_b = jnp.dot(o_ref[...], wa_ref[...], preferred_element_type=F32)
    sgu_b = jnp.dot(sgu_ref[...], ws_ref[...], preferred_element_type=F32)
    mix = (ga_ref[...].astype(F32) * att_b + gs_ref[...].astype(F32) * sgu_b).astype(BF16)
    h = x_ref[...] + jnp.dot(mix, wo_ref[...], preferred_element_type=F32)
    h_ref[...] = h
    ms = jnp.mean(h * h, axis=-1, keepdims=True)
    c_ref[...] = (h * lax.rsqrt(ms + EPS) * g2_ref[...]).astype(BF16)


def _merge(x2, o2, z2, sgu_w, bs_full, w_att_out, w_sgu_out, w_out, norm2_g):
    m = x2.shape[0]
    row = lambda i: (i, 0)
    const2 = lambda i: (0, 0)
    resident = functools.partial(pl.BlockSpec, pipeline_mode=pl.Buffered(1))
    zsec = lambda first, n: pl.BlockSpec((MRG_TM, n * SEC), lambda i: (i, first // n))
    vmem = ((SEC * D_MODEL * 2 + D_MODEL * D_MODEL) * 2
            + 2 * MRG_TM * (D_MODEL * 4 + 3 * SEC * 2 + 2 * D_MODEL * 2 + D_MODEL * 4 + D_MODEL * 2)
            + 6 * MRG_TM * D_MODEL * 4 + (6 << 20))
    return pl.pallas_call(
        _merge_kernel,
        name="merge",
        grid=(m // MRG_TM,),
        in_specs=[
            pl.BlockSpec((MRG_TM, D_MODEL), row),
            pl.BlockSpec((MRG_TM, SEC), row),
            zsec(Z_U, 1), zsec(Z_SV, 1), zsec(Z_GATT, 2), zsec(Z_GSGU, 2),
            resident((N_GROUPS, CHUNK, CHUNK), lambda i: (0, 0, 0)),
            resident((CHUNK, SEC), const2),
            resident((SEC, D_MODEL), const2),
            resident((SEC, D_MODEL), const2),
            resident((D_MODEL, D_MODEL), const2),
            pl.BlockSpec((1, D_MODEL), const2),
        ],
        out_specs=[pl.BlockSpec((MRG_TM, D_MODEL), row), pl.BlockSpec((MRG_TM, D_MODEL), row)],
        out_shape=[jax.ShapeDtypeStruct((m, D_MODEL), F32), jax.ShapeDtypeStruct((m, D_MODEL), BF16)],
        scratch_shapes=[pltpu.VMEM((MRG_TM, SEC), BF16)],
        compiler_params=pltpu.CompilerParams(
            dimension_semantics=("parallel",), vmem_limit_bytes=_vmem_limit(vmem)),
    )(x2, o2, z2, z2, z2, z2, sgu_w, bs_full, w_att_out, w_sgu_out, w_out, norm2_g)


def _ffn_kernel(c_ref, halo_ref, h_hbm, wg_ref, wv_ref, cwg_ref, cwv_ref, cbg_ref, cbv_ref, wd_ref,
                o_ref, cext_ref, ug0_ref, ug1_ref, uv0_ref, uv1_ref, act_ref, h_sem, *, blocks_per_seq):
    ug_refs, uv_refs = (ug0_ref, ug1_ref), (uv0_ref, uv1_ref)
    i = pl.program_id(0)
    j = pl.program_id(1)

    @pl.when(j == 0)
    def _():
        h_copy = pltpu.make_async_copy(h_hbm.at[pl.ds(i * FFN_TM, FFN_TM), :], o_ref, h_sem)
        h_copy.start()
        halo = halo_ref[...]
        cext_ref[pl.ds(0, FFN_HALO), :] = jnp.where(i % blocks_per_seq == 0, jnp.zeros_like(halo), halo)
        cext_ref[pl.ds(FFN_HALO, FFN_TM), :] = c_ref[...]
        h_copy.wait()

    def conv(u_ref, w_ref, b_ref, row0, cols):
        acc = b_ref[:, cols] + w_ref[pl.ds(CONV_WIDTH - 1, 1), cols] * u_ref[pl.ds(FFN_HALO + row0, FFN_RC), :]
        for tap in range(CONV_WIDTH - 1):
            shift = CONV_WIDTH - 1 - tap
            acc = acc + w_ref[pl.ds(tap, 1), cols] * u_ref[pl.ds(FFN_HALO + row0 - shift, FFN_RC), :]
        return acc

    half = FFN_TM // 2
    chunks_per_half = half // FFN_RC
    n_sub = FFN_TF // FFN_SUB
    pieces = [(sub, seg) for sub in range(n_sub) for seg in range(2)]

    def up_proj(sub, seg):
        cols = slice(sub * FFN_SUB, (sub + 1) * FFN_SUB)
        ext_rows = pl.ds(0, FFN_HALO + half) if seg == 0 else pl.ds(FFN_HALO + half, half)
        ug_refs[sub][ext_rows, :] = jnp.dot(cext_ref[ext_rows, :], wg_ref[:, cols], preferred_element_type=F32)
        uv_refs[sub][ext_rows, :] = jnp.dot(cext_ref[ext_rows, :], wv_ref[:, cols], preferred_element_type=F32)

    def activate(sub, seg):
        cols = slice(sub * FFN_SUB, (sub + 1) * FFN_SUB)
        for r in range(seg * chunks_per_half, (seg + 1) * chunks_per_half):
            row0 = r * FFN_RC
            gate = _gelu(conv(ug_refs[sub], cwg_ref, cbg_ref, row0, cols))
            act_ref[pl.ds(row0, FFN_RC), cols] = (
                gate * conv(uv_refs[sub], cwv_ref, cbv_ref, row0, cols)).astype(BF16)

    def down_proj(seg):
        for r in range(seg * chunks_per_half, (seg + 1) * chunks_per_half):
            rows = pl.ds(r * FFN_RC, FFN_RC)
            o_ref[rows, :] += jnp.dot(act_ref[rows, :], wd_ref[...], preferred_element_type=F32)

    up_proj(*pieces[0])
    for k, (sub, seg) in enumerate(pieces):
        if k + 1 < len(pieces):
            up_proj(*pieces[k + 1])
        activate(sub, seg)
        if sub == n_sub - 1:
            down_proj(seg)


def _ffn(c2, h2, w_up_bf, conv_w, conv_b, w_down, seq):
    m = c2.shape[0]
    nf = D_FF // FFN_TF
    blocks_per_seq = seq // FFN_TM
    halo_blocks = FFN_TM // FFN_HALO
    ext = FFN_TM + FFN_HALO
    row = lambda i, j: (i, 0)
    gate_col = lambda i, j: (0, j)
    val_col = lambda i, j: (0, nf + j)
    assert FFN_TF == 2 * FFN_SUB
    vmem = (2 * FFN_TM * D_MODEL * 2 + ext * D_MODEL * 2 + 2 * FFN_TM * D_MODEL * 4
            + 4 * D_MODEL * FFN_TF * 2 + 2 * FFN_TF * D_MODEL * 2
            + 2 * ext * FFN_TF * 4 + FFN_TM * FFN_TF * 2
            + 4 * ext * FFN_SUB * 4 + 6 * FFN_RC * D_MODEL * 4 + (2 << 20))
    return pl.pallas_call(
        functools.partial(_ffn_kernel, blocks_per_seq=blocks_per_seq),
        name="ffn",
        grid=(m // FFN_TM, nf),
        in_specs=[
            pl.BlockSpec((FFN_TM, D_MODEL), row),
            pl.BlockSpec((FFN_HALO, D_MODEL), lambda i, j: (jnp.maximum(i * halo_blocks - 1, 0), 0)),
            pl.BlockSpec(memory_space=pl.ANY),
            pl.BlockSpec((D_MODEL, FFN_TF), gate_col),
            pl.BlockSpec((D_MODEL, FFN_TF), val_col),
            pl.BlockSpec((CONV_WIDTH, FFN_TF), gate_col),
            pl.BlockSpec((CONV_WIDTH, FFN_TF), val_col),
            pl.BlockSpec((1, FFN_TF), gate_col),
            pl.BlockSpec((1, FFN_TF), val_col),
            pl.BlockSpec((FFN_TF, D_MODEL), lambda i, j: (j, 0)),
        ],
        out_specs=pl.BlockSpec((FFN_TM, D_MODEL), row),
        out_shape=jax.ShapeDtypeStruct((m, D_MODEL), F32),
        scratch_shapes=[
            pltpu.VMEM((ext, D_MODEL), BF16),
            *[pltpu.VMEM((ext, FFN_SUB), F32)] * (2 * (FFN_TF // FFN_SUB)),
            pltpu.VMEM((FFN_TM, FFN_TF), BF16),
            pltpu.SemaphoreType.DMA(()),
        ],
        compiler_params=pltpu.CompilerParams(
            dimension_semantics=("parallel", "arbitrary"), vmem_limit_bytes=_vmem_limit(vmem)),
    )(c2, c2, h2, w_up_bf, w_up_bf, conv_w, conv_w, conv_b, conv_b, w_down)


def _rope_tables(seq):
    inv = jnp.exp(-math.log(ROPE_THETA) * jnp.arange(0, HEAD_DIM, 2, dtype=F32) / HEAD_DIM)
    ang = jnp.arange(seq, dtype=F32)[:, None] * inv[None, :]
    cos, sin = jnp.cos(ang), jnp.sin(ang)
    return jnp.tile(cos, (1, 4)), jnp.tile(jnp.concatenate([-sin, sin], axis=1), (1, 2))


def kernel(x, norm1_g, w_in, b_gate, q_norm_g, k_norm_g, lambda_q1, lambda_k1, lambda_q2, lambda_k2, subln_g, sgu_norm_g, sgu_norm_b, sgu_w, sgu_b, w_att_out, w_sgu_out, w_out, norm2_g, w_up, conv_w, conv_b, w_down):
    bsz, seq, d = x.shape
    assert d == D_MODEL and w_in.shape[0] == 1
    assert seq % IN_TM == 0 and seq % FFN_TM == 0 and seq % ATT_TQ == 0 and seq % MRG_TM == 0
    m = bsz * seq
    x2 = x.reshape(m, d)
    cos_t, sin_t = _rope_tables(seq)
    seg = (jnp.arange(256)[:, None] // HEAD_DIM == jnp.arange(256)[None, :] // HEAD_DIM).astype(BF16)
    tile_heads = lambda g: jnp.tile(g, (1, SEC // HEAD_DIM))
    bs_full = jnp.repeat(sgu_b[0].T, GROUP_DIM, axis=1)

    z2, w_att_bf, w_sgu_bf, w_out_bf = _in_proj(
        x2, norm1_g, w_in[0].astype(BF16), b_gate, tile_heads(q_norm_g), tile_heads(k_norm_g),
        cos_t, sin_t, seg, sgu_norm_g, sgu_norm_b, seq, (w_att_out[0], w_sgu_out[0], w_out[0]))
    o3, w_up_bf, w_down_bf = _attention(z2.reshape(bsz, seq, N_SEC * SEC), lambda_q1, lambda_k1, lambda_q2,
                                           lambda_k2, subln_g, w_up[0], w_down[0])
    h2, c2 = _merge(x2, o3.reshape(m, SEC), z2, sgu_w[0], bs_full, w_att_bf, w_sgu_bf, w_out_bf, norm2_g)
    out = _ffn(c2, h2, w_up_bf, conv_w[0], conv_b, w_down_bf, seq)
    return out.reshape(bsz, seq, d)
```

```python
import functools
import math

import jax
import jax.numpy as jnp
from jax import lax
from jax.experimental import pallas as pl
from jax.experimental.pallas import tpu as pltpu

F32 = jnp.float32
BF16 = jnp.bfloat16

D_MODEL = 2048
N_HEADS = 8
HEAD_DIM = 64
V_DIM = 2 * HEAD_DIM
SEC = 1024
N_SEC = 9
N_GATE_SEC = 4
Z_GATT, Z_GSGU, Z_Q, Z_K, Z_V, Z_U, Z_SV = 0, 2, 4, 5, 6, 7, 8
ROPE_THETA = 10000.0
CHUNK = 128
N_GROUPS = 8
GROUP_DIM = 128
D_FF = 5632
CONV_WIDTH = 3
EPS = 1e-6
LAM_INIT = 0.8 - 0.6 * math.exp(-0.3 * 0)
SQRT_HALF = math.sqrt(0.5)
LOG2_E = math.log2(math.e)

V7X_VMEM_BYTES = 64 * 1024 * 1024
LANES = 128
BF16_SUBLANES = 16

IN_TM = 1024
IN_RC = 256
CAST_ROWS = 64
ATT_TQ = 256
ATT_RC = 16
MRG_TM = 256
FFN_TM = 1024
FFN_TF = 512
FFN_SUB = 256
FFN_SEGS = 2
FFN_RC = 256
FFN_HALO = BF16_SUBLANES


def _gelu(x):
    return 0.5 * x * (1.0 + lax.erf(x * SQRT_HALF))


def _vmem_limit(nbytes):
    return int(min(V7X_VMEM_BYTES - (4 << 20), nbytes))


def _in_proj_kernel(x_ref, g1_ref, w_ref, bg_ref, qg_ref, kg_ref, cos_ref, sin_ref, seg_ref,
                    lng_ref, lnb_ref, wa_ref, ws_ref, wo_ref, o_ref, wa_bf_ref, ws_bf_ref, wo_bf_ref, a_ref):
    wa_bf_ref[...] = wa_ref[...].astype(BF16)
    ws_bf_ref[...] = ws_ref[...].astype(BF16)
    wo_bf_ref[...] = wo_ref[...].astype(BF16)

    j = pl.program_id(1)
    n_chunks = IN_TM // IN_RC

    @pl.when(j == 0)
    def _():
        for r in range(n_chunks):
            rows = pl.ds(r * IN_RC, IN_RC)
            xs = x_ref[rows, :]
            ms = jnp.mean(xs * xs, axis=-1, keepdims=True)
            a_ref[rows, :] = (xs * lax.rsqrt(ms + EPS) * g1_ref[...]).astype(BF16)

    def for_chunks(epilogue):
        for r in range(n_chunks):
            rows = pl.ds(r * IN_RC, IN_RC)
            z = jnp.dot(a_ref[rows, :], w_ref[...], preferred_element_type=F32)
            o_ref[rows, :] = epilogue(z, rows).astype(BF16)

    def qk_epilogue(gain_ref, scale):
        def ep(z, rows):
            sq = (z * z).astype(BF16)
            ss = jnp.concatenate(
                [jnp.dot(sq[:, c * 256:(c + 1) * 256], seg_ref[...], preferred_element_type=F32)
                 for c in range(SEC // 256)], axis=1)
            rinv = lax.rsqrt(ss * (1.0 / HEAD_DIM) + EPS)
            zg = z * gain_ref[...]
            lane = lax.broadcasted_iota(jnp.int32, zg.shape, 1)
            partner = jnp.where((lane & 32) == 0,
                                pltpu.roll(zg, SEC - 32, axis=1), pltpu.roll(zg, 32, axis=1))
            cos = jnp.concatenate([cos_ref[rows, :]] * N_HEADS, axis=1)
            sin = jnp.concatenate([sin_ref[rows, :]] * N_HEADS, axis=1)
            out = (zg * cos + partner * sin) * rinv
            return out * scale if scale != 1.0 else out
        return ep

    @pl.when(j == 0)
    def _():
        for_chunks(qk_epilogue(qg_ref, HEAD_DIM ** -0.5 * LOG2_E))

    @pl.when(j == 1)
    def _():
        for_chunks(qk_epilogue(kg_ref, 1.0))

    @pl.when(j == 2)
    def _():
        for_chunks(lambda z, rows: z)

    @pl.when(j == 3)
    def _():
        for_chunks(lambda z, rows: _gelu(z))

    @pl.when(j == 4)
    def _():
        def ep(z, rows):
            gl = _gelu(z)
            mu = jnp.mean(gl, axis=-1, keepdims=True)
            xc = gl - mu
            var = jnp.mean(xc * xc, axis=-1, keepdims=True)
            return xc * lax.rsqrt(var + EPS) * lng_ref[...] + lnb_ref[...]
        for_chunks(ep)

    @pl.when(j >= 5)
    def _():
        for_chunks(lambda z, rows: jax.nn.sigmoid(z + bg_ref[...]))


def _in_proj(x2, norm1_g, w_in, b_gate, qg_t, kg_t, cos_t, sin_t, seg, ln_g, ln_b, seq, merge_weights):
    m = x2.shape[0]
    blocks_per_seq = seq // IN_TM
    row = lambda i, j: (i, 0)
    const = lambda i, j: (0, 0)
    cast_specs, first = [], 0
    for w in merge_weights:
        n_blocks = w.shape[0] // CAST_ROWS
        index = functools.partial(
            lambda i, j, first, n_blocks: (jnp.clip(i * N_SEC + j - first, 0, n_blocks - 1), 0),
            first=first, n_blocks=n_blocks)
        cast_specs.append(pl.BlockSpec((CAST_ROWS, w.shape[1]), index))
        first += n_blocks
    assert first <= (m // IN_TM) * N_SEC
    vmem = (2 * IN_TM * D_MODEL * 4 + 2 * D_MODEL * SEC * 2 + IN_TM * D_MODEL * 2
            + 2 * IN_TM * SEC * 2 + 12 * IN_RC * SEC * 4
            + 2 * len(merge_weights) * CAST_ROWS * D_MODEL * (4 + 2) + (4 << 20))
    return pl.pallas_call(
        _in_proj_kernel,
        name="in_proj",
        grid=(m // IN_TM, N_SEC),
        in_specs=[
            pl.BlockSpec((IN_TM, D_MODEL), row),
            pl.BlockSpec((1, D_MODEL), const),
            pl.BlockSpec((D_MODEL, SEC), lambda i, j: (0, j)),
            pl.BlockSpec((1, SEC), lambda i, j: (0, jnp.maximum(j - 5, 0))),
            pl.BlockSpec((1, SEC), const),
            pl.BlockSpec((1, SEC), const),
            pl.BlockSpec((IN_TM, LANES), lambda i, j: (i % blocks_per_seq, 0)),
            pl.BlockSpec((IN_TM, LANES), lambda i, j: (i % blocks_per_seq, 0)),
            pl.BlockSpec((256, 256), const),
            pl.BlockSpec((1, SEC), const),
            pl.BlockSpec((1, SEC), const),
            *cast_specs,
        ],
        out_specs=[pl.BlockSpec((IN_TM, SEC), lambda i, j: (i, (j + N_GATE_SEC) % N_SEC)), *cast_specs],
        out_shape=[jax.ShapeDtypeStruct((m, N_SEC * SEC), BF16),
                   *[jax.ShapeDtypeStruct(w.shape, BF16) for w in merge_weights]],
        scratch_shapes=[pltpu.VMEM((IN_TM, D_MODEL), BF16)],
        compiler_params=pltpu.CompilerParams(
            dimension_semantics=("arbitrary", "arbitrary"), vmem_limit_bytes=_vmem_limit(vmem)),
    )(x2, norm1_g, w_in, b_gate, qg_t, kg_t, cos_t, sin_t, seg, ln_g, ln_b, *merge_weights)


def _attn_kernel(lq1_ref, lk1_ref, lq2_ref, lk2_ref, subg_ref, q_ref, k_ref, v_ref, wu_ref, wd_ref,
                 o_ref, wu_bf_ref, wd_bf_ref, s_ref, p_ref, vext_ref, *, seq):
    wu_bf_ref[...] = wu_ref[...].astype(BF16)
    wd_bf_ref[...] = wd_ref[...].astype(BF16)

    tq = ATT_TQ
    lam = (jnp.exp(jnp.sum(lq1_ref[...] * lk1_ref[...], axis=-1, keepdims=True))
           - jnp.exp(jnp.sum(lq2_ref[...] * lk2_ref[...], axis=-1, keepdims=True)) + LAM_INIT)
    rc = ATT_RC
    lane = lax.broadcasted_iota(jnp.int32, (tq, V_DIM), 1)
    row_i = lax.broadcasted_iota(jnp.int32, (rc, tq), 0)
    col_i = lax.broadcasted_iota(jnp.int32, (rc, tq), 1)
    nt_dims = (((1,), (1,)), ((), ()))

    vext_ref[:, pl.ds(0, V_DIM)] = v_ref[0]
    vext_ref[:, pl.ds(V_DIM, V_DIM)] = jnp.ones((seq, V_DIM), BF16)

    def scores(qi, part):
        off = qi * tq
        q = q_ref[0, pl.ds(off, tq), :]
        qm = jnp.where((lane < HEAD_DIM) == (part == 0), q, jnp.zeros_like(q))
        s_ref[qi % 2, pl.ds(part * tq, tq), pl.ds(0, off + tq)] = lax.dot_general(
            qm, k_ref[0, pl.ds(0, off + tq), :], nt_dims, preferred_element_type=F32)

    order = list(range(seq // tq - 1, -1, -1))
    scores(order[0], 0)
    scores(order[0], 1)
    for pos, qi in enumerate(order):
        slot = qi % 2
        off = qi * tq
        kv_len = off + tq
        on = []
        for part in range(2):
            if pos + 1 < len(order):
                scores(order[pos + 1], part)
            for r in range(part * tq // rc, (part + 1) * tq // rc):
                rows = pl.ds(r * rc, rc)
                keep = ((r * rc) % tq + row_i) >= col_i
                sd = jnp.where(keep, s_ref[slot, rows, pl.ds(off, tq)], -jnp.inf)
                m = jnp.max(sd, axis=-1, keepdims=True)
                if off > 0:
                    so = s_ref[slot, rows, pl.ds(0, off)]
                    m = jnp.maximum(m, jnp.max(so, axis=-1, keepdims=True))
                    p_ref[slot, rows, pl.ds(0, off)] = jnp.exp2(so - m).astype(BF16)
                p_ref[slot, rows, pl.ds(off, tq)] = jnp.exp2(sd - m).astype(BF16)
            ol = jnp.dot(p_ref[slot, pl.ds(part * tq, tq), pl.ds(0, kv_len)], vext_ref[pl.ds(0, kv_len), :],
                         preferred_element_type=F32)
            on.append(ol[:, :V_DIM] / ol[:, V_DIM:])
        o = on[0] - lam * on[1]
        o = o * lax.rsqrt(jnp.mean(o * o, axis=-1, keepdims=True) + EPS) * subg_ref[...]
        o_ref[0, pl.ds(off, tq), :] = (o * (1.0 - LAM_INIT)).astype(BF16)


def _attention(z3, lq1, lk1, lq2, lk2, subln_g, w_up, w_down):
    b, seq, _ = z3.shape
    n_up_tiles = 2 * D_FF // FFN_TF
    n_cast = 2 * n_up_tiles
    dn_rows = D_FF // n_cast
    assert b * N_HEADS >= n_cast and D_FF % n_cast == 0 and dn_rows % BF16_SUBLANES == 0
    cast_id = lambda bi, h: jnp.minimum(bi * N_HEADS + h, n_cast - 1)
    small = lambda n: pl.BlockSpec((1, n), lambda bi, h: (0, 0))
    blk = lambda off: pl.BlockSpec((1, seq, V_DIM), lambda bi, h: (bi, 0, off + h))
    return pl.pallas_call(
        functools.partial(_attn_kernel, seq=seq),
        name="attn",
        grid=(b, N_HEADS),
        in_specs=[small(HEAD_DIM)] * 4 + [small(V_DIM), blk(Z_Q * N_HEADS), blk(Z_K * N_HEADS), blk(Z_V * N_HEADS),
                  pl.BlockSpec((D_MODEL // 2, FFN_TF), lambda bi, h: (cast_id(bi, h) % 2, cast_id(bi, h) // 2)),
                  pl.BlockSpec((dn_rows, D_MODEL), lambda bi, h: (cast_id(bi, h), 0))],
        out_specs=[pl.BlockSpec((1, seq, V_DIM), lambda bi, h: (bi, 0, h)),
                   pl.BlockSpec((D_MODEL // 2, FFN_TF), lambda bi, h: (cast_id(bi, h) % 2, cast_id(bi, h) // 2)),
                   pl.BlockSpec((dn_rows, D_MODEL), lambda bi, h: (cast_id(bi, h), 0))],
        out_shape=[jax.ShapeDtypeStruct((b, seq, N_HEADS * V_DIM), BF16),
                   jax.ShapeDtypeStruct((D_MODEL, 2 * D_FF), BF16),
                   jax.ShapeDtypeStruct((D_FF, D_MODEL), BF16)],
        scratch_shapes=[pltpu.VMEM((2, 2 * ATT_TQ, seq), F32), pltpu.VMEM((2, 2 * ATT_TQ, seq), BF16),
                        pltpu.VMEM((seq, 2 * V_DIM), BF16)],
        compiler_params=pltpu.CompilerParams(
            dimension_semantics=("arbitrary", "arbitrary"),
            vmem_limit_bytes=_vmem_limit(2 * 2 * ATT_TQ * seq * (4 + 2) + seq * 2 * V_DIM * 2
                                         + 2 * 4 * seq * V_DIM * 2
                                         + 2 * (D_MODEL // 2 * FFN_TF + dn_rows * D_MODEL) * (4 + 2) + (16 << 20))),
    )(lq1, lk1, lq2, lk2, subln_g, z3, z3, z3, w_up, w_down)


def _merge_kernel(x_ref, o_ref, u_ref, sv_ref, ga_ref, gs_ref, wsp_ref, bsp_ref, wa_ref, ws_ref, wo_ref,
                  g2_ref, h_ref, c_ref, sgu_ref):
    t_idx = lax.broadcasted_iota(jnp.int32, (CHUNK, CHUNK), 0)
    p_idx = lax.broadcasted_iota(jnp.int32, (CHUNK, CHUNK), 1)
    for g in range(N_GROUPS):
        cols = slice(g * GROUP_DIM, (g + 1) * GROUP_DIM)
        w = jnp.where(t_idx >= p_idx, wsp_ref[g], 0.0).astype(BF16)
        for c in range(MRG_TM // CHUNK):
            rows = pl.ds(c * CHUNK, CHUNK)
            s = jnp.dot(w, sv_ref[rows, cols], preferred_element_type=F32) + bsp_ref[:, cols]
            sgu_ref[rows, cols] = (u_ref[rows, cols].astype(F32) * s).astype(BF16)
    att_b = jnp.dot(o_ref[...], wa_ref[...], preferred_element_type=F32)
    sgu_b = jnp.dot(sgu_ref[...], ws_ref[...], preferred_element_type=F32)
    mix = (ga_ref[...].astype(F32) * att_b + gs_ref[...].astype(F32) * sgu_b).astype(BF16)
    h = x_ref[...] + jnp.dot(mix, wo_ref[...], preferred_element_type=F32)
    h_ref[...] = h
    ms = jnp.mean(h * h, axis=-1, keepdims=True)
    c_ref[...] = (h * lax.rsqrt(ms + EPS) * g2_ref[...]).astype(BF16)


def _merge(x2, o2, z2, sgu_w, bs_full, w_att_out, w_sgu_out, w_out, norm2_g):
    m = x2.shape[0]
    row = lambda i: (i, 0)
    const2 = lambda i: (0, 0)
    resident = functools.partial(pl.BlockSpec, pipeline_mode=pl.Buffered(1))
    zsec = lambda first, n: pl.BlockSpec((MRG_TM, n * SEC), lambda i: (i, first // n))
    vmem = ((SEC * D_MODEL * 2 + D_MODEL * D_MODEL) * 2
            + 2 * MRG_TM * (D_MODEL * 4 + 3 * SEC * 2 + 2 * D_MODEL * 2 + D_MODEL * 4 + D_MODEL * 2)
            + 6 * MRG_TM * D_MODEL * 4 + (6 << 20))
    return pl.pallas_call(
        _merge_kernel,
        name="merge",
        grid=(m // MRG_TM,),
        in_specs=[
            pl.BlockSpec((MRG_TM, D_MODEL), row),
            pl.BlockSpec((MRG_TM, SEC), row),
            zsec(Z_U, 1), zsec(Z_SV, 1), zsec(Z_GATT, 2), zsec(Z_GSGU, 2),
            resident((N_GROUPS, CHUNK, CHUNK), lambda i: (0, 0, 0)),
            resident((CHUNK, SEC), const2),
            resident((SEC, D_MODEL), const2),
            resident((SEC, D_MODEL), const2),
            resident((D_MODEL, D_MODEL), const2),
            pl.BlockSpec((1, D_MODEL), const2),
        ],
        out_specs=[pl.BlockSpec((MRG_TM, D_MODEL), row), pl.BlockSpec((MRG_TM, D_MODEL), row)],
        out_shape=[jax.ShapeDtypeStruct((m, D_MODEL), F32), jax.ShapeDtypeStruct((m, D_MODEL), BF16)],
        scratch_shapes=[pltpu.VMEM((MRG_TM, SEC), BF16)],
        compiler_params=pltpu.CompilerParams(
            dimension_semantics=("parallel",), vmem_limit_bytes=_vmem_limit(vmem)),
    )(x2, o2, z2, z2, z2, z2, sgu_w, bs_full, w_att_out, w_sgu_out, w_out, norm2_g)


def _ffn_kernel(c_ref, halo_ref, h_hbm, wg_ref, wv_ref, cwg_ref, cwv_ref, cbg_ref, cbv_ref, wd_ref,
                o_ref, cext_ref, *scratch, blocks_per_seq):
    n_sub = FFN_TF // FFN_SUB
    ug_refs, uv_refs, (act_ref, h_sem) = scratch[:n_sub], scratch[n_sub:2 * n_sub], scratch[2 * n_sub:]
    i = pl.program_id(0)
    j = pl.program_id(1)

    @pl.when(j == 0)
    def _():
        h_copy = pltpu.make_async_copy(h_hbm.at[pl.ds(i * FFN_TM, FFN_TM), :], o_ref, h_sem)
        h_copy.start()
        halo = halo_ref[...]
        cext_ref[pl.ds(0, FFN_HALO), :] = jnp.where(i % blocks_per_seq == 0, jnp.zeros_like(halo), halo)
        cext_ref[pl.ds(FFN_HALO, FFN_TM), :] = c_ref[...]
        h_copy.wait()

    def conv(u_ref, w_ref, b_ref, row0, cols):
        acc = b_ref[:, cols] + w_ref[pl.ds(CONV_WIDTH - 1, 1), cols] * u_ref[pl.ds(FFN_HALO + row0, FFN_RC), :]
        for tap in range(CONV_WIDTH - 1):
            shift = CONV_WIDTH - 1 - tap
            acc = acc + w_ref[pl.ds(tap, 1), cols] * u_ref[pl.ds(FFN_HALO + row0 - shift, FFN_RC), :]
        return acc

    seg_rows = FFN_TM // FFN_SEGS
    chunks_per_half = seg_rows // FFN_RC
    pieces = [(sub, seg) for sub in range(n_sub) for seg in range(FFN_SEGS)]

    def up_proj(sub, seg):
        cols = slice(sub * FFN_SUB, (sub + 1) * FFN_SUB)
        ext_rows = pl.ds(0, FFN_HALO + seg_rows) if seg == 0 else pl.ds(FFN_HALO + seg * seg_rows, seg_rows)
        ug_refs[sub][ext_rows, :] = jnp.dot(cext_ref[ext_rows, :], wg_ref[:, cols], preferred_element_type=F32)
        uv_refs[sub][ext_rows, :] = jnp.dot(cext_ref[ext_rows, :], wv_ref[:, cols], preferred_element_type=F32)

    def activate(sub, seg):
        cols = slice(sub * FFN_SUB, (sub + 1) * FFN_SUB)
        for r in range(seg * chunks_per_half, (seg + 1) * chunks_per_half):
            row0 = r * FFN_RC
            gate = _gelu(conv(ug_refs[sub], cwg_ref, cbg_ref, row0, cols))
            act_ref[pl.ds(row0, FFN_RC), cols] = (
                gate * conv(uv_refs[sub], cwv_ref, cbv_ref, row0, cols)).astype(BF16)

    def down_proj(seg):
        for r in range(seg * chunks_per_half, (seg + 1) * chunks_per_half):
            rows = pl.ds(r * FFN_RC, FFN_RC)
            o_ref[rows, :] += jnp.dot(act_ref[rows, :], wd_ref[...], preferred_element_type=F32)

    up_proj(*pieces[0])
    for k, (sub, seg) in enumerate(pieces):
        if k + 1 < len(pieces):
            up_proj(*pieces[k + 1])
        activate(sub, seg)
        if sub == n_sub - 1:
            down_proj(seg)


def _ffn(c2, h2, w_up_bf, conv_w, conv_b, w_down, seq):
    m = c2.shape[0]
    nf = D_FF // FFN_TF
    blocks_per_seq = seq // FFN_TM
    halo_blocks = FFN_TM // FFN_HALO
    ext = FFN_TM + FFN_HALO
    row = lambda i, j: (i, 0)
    gate_col = lambda i, j: (0, j)
    val_col = lambda i, j: (0, nf + j)
    assert FFN_TF % FFN_SUB == 0 and FFN_TM % (FFN_SEGS * FFN_RC) == 0
    vmem = (2 * FFN_TM * D_MODEL * 2 + ext * D_MODEL * 2 + 2 * FFN_TM * D_MODEL * 4
            + 4 * D_MODEL * FFN_TF * 2 + 2 * FFN_TF * D_MODEL * 2
            + 2 * ext * FFN_TF * 4 + FFN_TM * FFN_TF * 2
            + 4 * ext * FFN_SUB * 4 + 6 * FFN_RC * D_MODEL * 4 + (2 << 20))
    return pl.pallas_call(
        functools.partial(_ffn_kernel, blocks_per_seq=blocks_per_seq),
        name="ffn",
        grid=(m // FFN_TM, nf),
        in_specs=[
            pl.BlockSpec((FFN_TM, D_MODEL), row),
            pl.BlockSpec((FFN_HALO, D_MODEL), lambda i, j: (jnp.maximum(i * halo_blocks - 1, 0), 0)),
            pl.BlockSpec(memory_space=pl.ANY),
            pl.BlockSpec((D_MODEL, FFN_TF), gate_col),
            pl.BlockSpec((D_MODEL, FFN_TF), val_col),
            pl.BlockSpec((CONV_WIDTH, FFN_TF), gate_col),
            pl.BlockSpec((CONV_WIDTH, FFN_TF), val_col),
            pl.BlockSpec((1, FFN_TF), gate_col),
            pl.BlockSpec((1, FFN_TF), val_col),
            pl.BlockSpec((FFN_TF, D_MODEL), lambda i, j: (j, 0)),
        ],
        out_specs=pl.BlockSpec((FFN_TM, D_MODEL), row),
        out_shape=jax.ShapeDtypeStruct((m, D_MODEL), F32),
        scratch_shapes=[
            pltpu.VMEM((ext, D_MODEL), BF16),
            *[pltpu.VMEM((ext, FFN_SUB), F32)] * (2 * (FFN_TF // FFN_SUB)),
            pltpu.VMEM((FFN_TM, FFN_TF), BF16),
            pltpu.SemaphoreType.DMA(()),
        ],
        compiler_params=pltpu.CompilerParams(
            dimension_semantics=("parallel", "arbitrary"), vmem_limit_bytes=_vmem_limit(vmem)),
    )(c2, c2, h2, w_up_bf, w_up_bf, conv_w, conv_w, conv_b, conv_b, w_down)


def _rope_tables(seq):
    inv = jnp.exp(-math.log(ROPE_THETA) * jnp.arange(0, HEAD_DIM, 2, dtype=F32) / HEAD_DIM)
    ang = jnp.arange(seq, dtype=F32)[:, None] * inv[None, :]
    cos, sin = jnp.cos(ang), jnp.sin(ang)
    return jnp.tile(cos, (1, 4)), jnp.tile(jnp.concatenate([-sin, sin], axis=1), (1, 2))


def kernel(x, norm1_g, w_in, b_gate, q_norm_g, k_norm_g, lambda_q1, lambda_k1, lambda_q2, lambda_k2, subln_g, sgu_norm_g, sgu_norm_b, sgu_w, sgu_b, w_att_out, w_sgu_out, w_out, norm2_g, w_up, conv_w, conv_b, w_down):
    bsz, seq, d = x.shape
    assert d == D_MODEL and w_in.shape[0] == 1
    assert seq % IN_TM == 0 and seq % FFN_TM == 0 and seq % ATT_TQ == 0 and seq % MRG_TM == 0
    m = bsz * seq
    x2 = x.reshape(m, d)
    cos_t, sin_t = _rope_tables(seq)
    seg = (jnp.arange(256)[:, None] // HEAD_DIM == jnp.arange(256)[None, :] // HEAD_DIM).astype(BF16)
    tile_heads = lambda g: jnp.tile(g, (1, SEC // HEAD_DIM))
    bs_full = jnp.repeat(sgu_b[0].T, GROUP_DIM, axis=1)

    z2, w_att_bf, w_sgu_bf, w_out_bf = _in_proj(
        x2, norm1_g, w_in[0].astype(BF16), b_gate, tile_heads(q_norm_g), tile_heads(k_norm_g),
        cos_t, sin_t, seg, sgu_norm_g, sgu_norm_b, seq, (w_att_out[0], w_sgu_out[0], w_out[0]))
    o3, w_up_bf, w_down_bf = _attention(z2.reshape(bsz, seq, N_SEC * SEC), lambda_q1, lambda_k1, lambda_q2,
                                           lambda_k2, subln_g, w_up[0], w_down[0])
    h2, c2 = _merge(x2, o3.reshape(m, SEC), z2, sgu_w[0], bs_full, w_att_bf, w_sgu_bf, w_out_bf, norm2_g)
    out = _ffn(c2, h2, w_up_bf, conv_w[0], conv_b, w_down_bf, seq)
    return out.reshape(bsz, seq, d)
```

```python
import functools
import math

import jax
import jax.numpy as jnp
from jax import lax
from jax.experimental import pallas as pl
from jax.experimental.pallas import tpu as pltpu

F32 = jnp.float32
BF16 = jnp.bfloat16

D_MODEL = 2048
N_HEADS = 8
HEAD_DIM = 64
V_DIM = 2 * HEAD_DIM
SEC = 1024
N_SEC = 9
N_GATE_SEC = 4
Z_GATT, Z_GSGU, Z_Q, Z_K, Z_V, Z_U, Z_SV = 0, 2, 4, 5, 6, 7, 8
ROPE_THETA = 10000.0
CHUNK = 128
N_GROUPS = 8
GROUP_DIM = 128
D_FF = 5632
CONV_WIDTH = 3
EPS = 1e-6
LAM_INIT = 0.8 - 0.6 * math.exp(-0.3 * 0)
SQRT_HALF = math.sqrt(0.5)
LOG2_E = math.log2(math.e)

V7X_VMEM_BYTES = 64 * 1024 * 1024
LANES = 128
BF16_SUBLANES = 16

IN_TM = 1024
IN_RC = 256
CAST_ROWS = 256
ATT_TQ = 256
ATT_RC = 16
MRG_TM = 256
FFN_TM = 1024
FFN_TF = 512
FFN_SUB = 256
FFN_SEGS = 2
FFN_RC = 256
FFN_HALO = BF16_SUBLANES


def _gelu(x):
    return 0.5 * x * (1.0 + lax.erf(x * SQRT_HALF))


def _vmem_limit(nbytes):
    return int(min(V7X_VMEM_BYTES - (4 << 20), nbytes))


def _in_proj_kernel(x_ref, g1_ref, w_ref, bg_ref, qg_ref, kg_ref, cos_ref, sin_ref, seg_ref,
                    lng_ref, lnb_ref, o_ref, a_ref):
    j = pl.program_id(1)
    n_chunks = IN_TM // IN_RC

    @pl.when(j == 0)
    def _():
        for r in range(n_chunks):
            rows = pl.ds(r * IN_RC, IN_RC)
            xs = x_ref[rows, :]
            ms = jnp.mean(xs * xs, axis=-1, keepdims=True)
            a_ref[rows, :] = (xs * lax.rsqrt(ms + EPS) * g1_ref[...]).astype(BF16)

    def for_chunks(epilogue):
        for r in range(n_chunks):
            rows = pl.ds(r * IN_RC, IN_RC)
            z = jnp.dot(a_ref[rows, :], w_ref[...], preferred_element_type=F32)
            o_ref[rows, :] = epilogue(z, rows).astype(BF16)

    def qk_epilogue(gain_ref, scale):
        def ep(z, rows):
            sq = (z * z).astype(BF16)
            ss = jnp.concatenate(
                [jnp.dot(sq[:, c * 256:(c + 1) * 256], seg_ref[...], preferred_element_type=F32)
                 for c in range(SEC // 256)], axis=1)
            rinv = lax.rsqrt(ss * (1.0 / HEAD_DIM) + EPS)
            zg = z * gain_ref[...]
            lane = lax.broadcasted_iota(jnp.int32, zg.shape, 1)
            partner = jnp.where((lane & 32) == 0,
                                pltpu.roll(zg, SEC - 32, axis=1), pltpu.roll(zg, 32, axis=1))
            cos = jnp.concatenate([cos_ref[rows, :]] * N_HEADS, axis=1)
            sin = jnp.concatenate([sin_ref[rows, :]] * N_HEADS, axis=1)
            out = (zg * cos + partner * sin) * rinv
            return out * scale if scale != 1.0 else out
        return ep

    @pl.when(j == 0)
    def _():
        for_chunks(qk_epilogue(qg_ref, HEAD_DIM ** -0.5 * LOG2_E))

    @pl.when(j == 1)
    def _():
        for_chunks(qk_epilogue(kg_ref, 1.0))

    @pl.when(j == 2)
    def _():
        for_chunks(lambda z, rows: z)

    @pl.when(j == 3)
    def _():
        for_chunks(lambda z, rows: _gelu(z))

    @pl.when(j == 4)
    def _():
        def ep(z, rows):
            gl = _gelu(z)
            mu = jnp.mean(gl, axis=-1, keepdims=True)
            xc = gl - mu
            var = jnp.mean(xc * xc, axis=-1, keepdims=True)
            return xc * lax.rsqrt(var + EPS) * lng_ref[...] + lnb_ref[...]
        for_chunks(ep)

    @pl.when(j >= 5)
    def _():
        for_chunks(lambda z, rows: jax.nn.sigmoid(z + bg_ref[...]))


def _in_proj(x2, norm1_g, w_in, b_gate, qg_t, kg_t, cos_t, sin_t, seg, ln_g, ln_b, seq):
    m = x2.shape[0]
    blocks_per_seq = seq // IN_TM
    row = lambda i, j: (i, 0)
    const = lambda i, j: (0, 0)
    vmem = (2 * IN_TM * D_MODEL * 4 + 2 * D_MODEL * SEC * 2 + IN_TM * D_MODEL * 2
            + 2 * IN_TM * SEC * 2 + 12 * IN_RC * SEC * 4 + (4 << 20))
    return pl.pallas_call(
        _in_proj_kernel,
        name="in_proj",
        grid=(m // IN_TM, N_SEC),
        in_specs=[
            pl.BlockSpec((IN_TM, D_MODEL), row),
            pl.BlockSpec((1, D_MODEL), const),
            pl.BlockSpec((D_MODEL, SEC), lambda i, j: (0, j)),
            pl.BlockSpec((1, SEC), lambda i, j: (0, jnp.maximum(j - 5, 0))),
            pl.BlockSpec((1, SEC), const),
            pl.BlockSpec((1, SEC), const),
            pl.BlockSpec((IN_TM, LANES), lambda i, j: (i % blocks_per_seq, 0)),
            pl.BlockSpec((IN_TM, LANES), lambda i, j: (i % blocks_per_seq, 0)),
            pl.BlockSpec((256, 256), const),
            pl.BlockSpec((1, SEC), const),
            pl.BlockSpec((1, SEC), const),
        ],
        out_specs=pl.BlockSpec((IN_TM, SEC), lambda i, j: (i, (j + N_GATE_SEC) % N_SEC)),
        out_shape=jax.ShapeDtypeStruct((m, N_SEC * SEC), BF16),
        scratch_shapes=[pltpu.VMEM((IN_TM, D_MODEL), BF16)],
        compiler_params=pltpu.CompilerParams(
            dimension_semantics=("parallel", "arbitrary"), vmem_limit_bytes=_vmem_limit(vmem)),
    )(x2, norm1_g, w_in, b_gate, qg_t, kg_t, cos_t, sin_t, seg, ln_g, ln_b)


def _attn_kernel(lq1_ref, lk1_ref, lq2_ref, lk2_ref, subg_ref, q_ref, k_ref, v_ref, wu_ref, wd_ref,
                 wa_ref, ws_ref, wo_ref, o_ref, wu_bf_ref, wd_bf_ref, wa_bf_ref, ws_bf_ref, wo_bf_ref,
                 s_ref, p_ref, vext_ref, *, seq, merge_cast_steps):
    wu_bf_ref[...] = wu_ref[...].astype(BF16)
    wd_bf_ref[...] = wd_ref[...].astype(BF16)
    step = pl.program_id(0) * N_HEADS + pl.program_id(1)
    for src, dst, (first, n_blocks) in zip((wa_ref, ws_ref, wo_ref), (wa_bf_ref, ws_bf_ref, wo_bf_ref),
                                           merge_cast_steps):
        @pl.when((step >= first) & (step < first + n_blocks))
        def _(src=src, dst=dst):
            dst[...] = src[...].astype(BF16)

    tq = ATT_TQ
    lam = (jnp.exp(jnp.sum(lq1_ref[...] * lk1_ref[...], axis=-1, keepdims=True))
           - jnp.exp(jnp.sum(lq2_ref[...] * lk2_ref[...], axis=-1, keepdims=True)) + LAM_INIT)
    rc = ATT_RC
    lane = lax.broadcasted_iota(jnp.int32, (tq, V_DIM), 1)
    row_i = lax.broadcasted_iota(jnp.int32, (rc, tq), 0)
    col_i = lax.broadcasted_iota(jnp.int32, (rc, tq), 1)
    nt_dims = (((1,), (1,)), ((), ()))

    vext_ref[:, pl.ds(0, V_DIM)] = v_ref[0]
    vext_ref[:, pl.ds(V_DIM, V_DIM)] = jnp.ones((seq, V_DIM), BF16)

    def scores(qi, part):
        off = qi * tq
        q = q_ref[0, pl.ds(off, tq), :]
        qm = jnp.where((lane < HEAD_DIM) == (part == 0), q, jnp.zeros_like(q))
        s_ref[qi % 2, pl.ds(part * tq, tq), pl.ds(0, off + tq)] = lax.dot_general(
            qm, k_ref[0, pl.ds(0, off + tq), :], nt_dims, preferred_element_type=F32)

    order = list(range(seq // tq - 1, -1, -1))
    scores(order[0], 0)
    scores(order[0], 1)
    for pos, qi in enumerate(order):
        slot = qi % 2
        off = qi * tq
        kv_len = off + tq
        on = []
        for part in range(2):
            if pos + 1 < len(order):
                scores(order[pos + 1], part)
            for r in range(part * tq // rc, (part + 1) * tq // rc):
                rows = pl.ds(r * rc, rc)
                keep = ((r * rc) % tq + row_i) >= col_i
                sd = jnp.where(keep, s_ref[slot, rows, pl.ds(off, tq)], -jnp.inf)
                m = jnp.max(sd, axis=-1, keepdims=True)
                if off > 0:
                    so = s_ref[slot, rows, pl.ds(0, off)]
                    m = jnp.maximum(m, jnp.max(so, axis=-1, keepdims=True))
                    p_ref[slot, rows, pl.ds(0, off)] = jnp.exp2(so - m).astype(BF16)
                p_ref[slot, rows, pl.ds(off, tq)] = jnp.exp2(sd - m).astype(BF16)
            ol = jnp.dot(p_ref[slot, pl.ds(part * tq, tq), pl.ds(0, kv_len)], vext_ref[pl.ds(0, kv_len), :],
                         preferred_element_type=F32)
            on.append(ol[:, :V_DIM] / ol[:, V_DIM:])
        o = on[0] - lam * on[1]
        o = o * lax.rsqrt(jnp.mean(o * o, axis=-1, keepdims=True) + EPS) * subg_ref[...]
        o_ref[0, pl.ds(off, tq), :] = (o * (1.0 - LAM_INIT)).astype(BF16)


def _attention(z3, lq1, lk1, lq2, lk2, subln_g, w_up, w_down, merge_weights):
    b, seq, _ = z3.shape
    n_up_tiles = 2 * D_FF // FFN_TF
    n_cast = 2 * n_up_tiles
    dn_rows = D_FF // n_cast
    assert D_FF % n_cast == 0 and dn_rows % BF16_SUBLANES == 0
    cast_id = lambda bi, h: jnp.minimum(bi * N_HEADS + h, n_cast - 1)
    merge_specs, merge_cast_steps, first = [], [], n_cast
    for w in merge_weights:
        n_blocks = w.shape[0] // CAST_ROWS
        index = functools.partial(
            lambda bi, h, first, n_blocks: (jnp.clip(bi * N_HEADS + h - first, 0, n_blocks - 1), 0),
            first=first, n_blocks=n_blocks)
        merge_specs.append(pl.BlockSpec((CAST_ROWS, w.shape[1]), index))
        merge_cast_steps.append((first, n_blocks))
        first += n_blocks
    assert first <= b * N_HEADS
    small = lambda n: pl.BlockSpec((1, n), lambda bi, h: (0, 0))
    blk = lambda off: pl.BlockSpec((1, seq, V_DIM), lambda bi, h: (bi, 0, off + h))
    return pl.pallas_call(
        functools.partial(_attn_kernel, seq=seq, merge_cast_steps=tuple(merge_cast_steps)),
        name="attn",
        grid=(b, N_HEADS),
        in_specs=[small(HEAD_DIM)] * 4 + [small(V_DIM), blk(Z_Q * N_HEADS), blk(Z_K * N_HEADS), blk(Z_V * N_HEADS),
                  pl.BlockSpec((D_MODEL // 2, FFN_TF), lambda bi, h: (cast_id(bi, h) % 2, cast_id(bi, h) // 2)),
                  pl.BlockSpec((dn_rows, D_MODEL), lambda bi, h: (cast_id(bi, h), 0)), *merge_specs],
        out_specs=[pl.BlockSpec((1, seq, V_DIM), lambda bi, h: (bi, 0, h)),
                   pl.BlockSpec((D_MODEL // 2, FFN_TF), lambda bi, h: (cast_id(bi, h) % 2, cast_id(bi, h) // 2)),
                   pl.BlockSpec((dn_rows, D_MODEL), lambda bi, h: (cast_id(bi, h), 0)), *merge_specs],
        out_shape=[jax.ShapeDtypeStruct((b, seq, N_HEADS * V_DIM), BF16),
                   jax.ShapeDtypeStruct((D_MODEL, 2 * D_FF), BF16),
                   jax.ShapeDtypeStruct((D_FF, D_MODEL), BF16),
                   *[jax.ShapeDtypeStruct(w.shape, BF16) for w in merge_weights]],
        scratch_shapes=[pltpu.VMEM((2, 2 * ATT_TQ, seq), F32), pltpu.VMEM((2, 2 * ATT_TQ, seq), BF16),
                        pltpu.VMEM((seq, 2 * V_DIM), BF16)],
        compiler_params=pltpu.CompilerParams(
            dimension_semantics=("arbitrary", "arbitrary"),
            vmem_limit_bytes=_vmem_limit(2 * 2 * ATT_TQ * seq * (4 + 2) + seq * 2 * V_DIM * 2
                                         + 2 * 4 * seq * V_DIM * 2
                                         + 2 * (D_MODEL // 2 * FFN_TF + dn_rows * D_MODEL) * (4 + 2)
                                         + 2 * len(merge_weights) * CAST_ROWS * D_MODEL * (4 + 2) + (16 << 20))),
    )(lq1, lk1, lq2, lk2, subln_g, z3, z3, z3, w_up, w_down, *merge_weights)


def _merge_kernel(x_ref, o_ref, u_ref, sv_ref, ga_ref, gs_ref, wsp_ref, bsp_ref, wa_ref, ws_ref, wo_ref,
                  g2_ref, h_ref, c_ref, sgu_ref):
    t_idx = lax.broadcasted_iota(jnp.int32, (CHUNK, CHUNK), 0)
    p_idx = lax.broadcasted_iota(jnp.int32, (CHUNK, CHUNK), 1)
    for g in range(N_GROUPS):
        cols = slice(g * GROUP_DIM, (g + 1) * GROUP_DIM)
        w = jnp.where(t_idx >= p_idx, wsp_ref[g], 0.0).astype(BF16)
        for c in range(MRG_TM // CHUNK):
            rows = pl.ds(c * CHUNK, CHUNK)
            s = jnp.dot(w, sv_ref[rows, cols], preferred_element_type=F32) + bsp_ref[:, cols]
            sgu_ref[rows, cols] = (u_ref[rows, cols].astype(F32) * s).astype(BF16)
    att_b = jnp.dot(o_ref[...], wa_ref[...], preferred_element_type=F32)
    sgu_b = jnp.dot(sgu_ref[...], ws_ref[...], preferred_element_type=F32)
    mix = (ga_ref[...].astype(F32) * att_b + gs_ref[...].astype(F32) * sgu_b).astype(BF16)
    h = x_ref[...] + jnp.dot(mix, wo_ref[...], preferred_element_type=F32)
    h_ref[...] = h
    ms = jnp.mean(h * h, axis=-1, keepdims=True)
    c_ref[...] = (h * lax.rsqrt(ms + EPS) * g2_ref[...]).astype(BF16)


def _merge(x2, o2, z2, sgu_w, bs_full, w_att_out, w_sgu_out, w_out, norm2_g):
    m = x2.shape[0]
    row = lambda i: (i, 0)
    const2 = lambda i: (0, 0)
    resident = functools.partial(pl.BlockSpec, pipeline_mode=pl.Buffered(1))
    zsec = lambda first, n: pl.BlockSpec((MRG_TM, n * SEC), lambda i: (i, first // n))
    vmem = ((SEC * D_MODEL * 2 + D_MODEL * D_MODEL) * 2
            + 2 * MRG_TM * (D_MODEL * 4 + 3 * SEC * 2 + 2 * D_MODEL * 2 + D_MODEL * 4 + D_MODEL * 2)
            + 6 * MRG_TM * D_MODEL * 4 + (6 << 20))
    return pl.pallas_call(
        _merge_kernel,
        name="merge",
        grid=(m // MRG_TM,),
        in_specs=[
            pl.BlockSpec((MRG_TM, D_MODEL), row),
            pl.BlockSpec((MRG_TM, SEC), row),
            zsec(Z_U, 1), zsec(Z_SV, 1), zsec(Z_GATT, 2), zsec(Z_GSGU, 2),
            resident((N_GROUPS, CHUNK, CHUNK), lambda i: (0, 0, 0)),
            resident((CHUNK, SEC), const2),
            resident((SEC, D_MODEL), const2),
            resident((SEC, D_MODEL), const2),
            resident((D_MODEL, D_MODEL), const2),
            pl.BlockSpec((1, D_MODEL), const2),
        ],
        out_specs=[pl.BlockSpec((MRG_TM, D_MODEL), row), pl.BlockSpec((MRG_TM, D_MODEL), row)],
        out_shape=[jax.ShapeDtypeStruct((m, D_MODEL), F32), jax.ShapeDtypeStruct((m, D_MODEL), BF16)],
        scratch_shapes=[pltpu.VMEM((MRG_TM, SEC), BF16)],
        compiler_params=pltpu.CompilerParams(
            dimension_semantics=("parallel",), vmem_limit_bytes=_vmem_limit(vmem)),
    )(x2, o2, z2, z2, z2, z2, sgu_w, bs_full, w_att_out, w_sgu_out, w_out, norm2_g)


def _ffn_kernel(c_ref, halo_ref, h_hbm, wg_ref, wv_ref, cwg_ref, cwv_ref, cbg_ref, cbv_ref, wd_ref,
                o_ref, cext_ref, *scratch, blocks_per_seq):
    n_sub = FFN_TF // FFN_SUB
    ug_refs, uv_refs, (act_ref, h_sem) = scratch[:n_sub], scratch[n_sub:2 * n_sub], scratch[2 * n_sub:]
    i = pl.program_id(0)
    j = pl.program_id(1)

    @pl.when(j == 0)
    def _():
        h_copy = pltpu.make_async_copy(h_hbm.at[pl.ds(i * FFN_TM, FFN_TM), :], o_ref, h_sem)
        h_copy.start()
        halo = halo_ref[...]
        cext_ref[pl.ds(0, FFN_HALO), :] = jnp.where(i % blocks_per_seq == 0, jnp.zeros_like(halo), halo)
        cext_ref[pl.ds(FFN_HALO, FFN_TM), :] = c_ref[...]
        h_copy.wait()

    def conv(u_ref, w_ref, b_ref, row0, cols):
        acc = b_ref[:, cols] + w_ref[pl.ds(CONV_WIDTH - 1, 1), cols] * u_ref[pl.ds(FFN_HALO + row0, FFN_RC), :]
        for tap in range(CONV_WIDTH - 1):
            shift = CONV_WIDTH - 1 - tap
            acc = acc + w_ref[pl.ds(tap, 1), cols] * u_ref[pl.ds(FFN_HALO + row0 - shift, FFN_RC), :]
        return acc

    seg_rows = FFN_TM // FFN_SEGS
    chunks_per_half = seg_rows // FFN_RC
    pieces = [(sub, seg) for sub in range(n_sub) for seg in range(FFN_SEGS)]

    def up_proj(sub, seg):
        cols = slice(sub * FFN_SUB, (sub + 1) * FFN_SUB)
        ext_rows = pl.ds(0, FFN_HALO + seg_rows) if seg == 0 else pl.ds(FFN_HALO + seg * seg_rows, seg_rows)
        ug_refs[sub][ext_rows, :] = jnp.dot(cext_ref[ext_rows, :], wg_ref[:, cols], preferred_element_type=F32)
        uv_refs[sub][ext_rows, :] = jnp.dot(cext_ref[ext_rows, :], wv_ref[:, cols], preferred_element_type=F32)

    def activate(sub, seg):
        cols = slice(sub * FFN_SUB, (sub + 1) * FFN_SUB)
        for r in range(seg * chunks_per_half, (seg + 1) * chunks_per_half):
            row0 = r * FFN_RC
            gate = _gelu(conv(ug_refs[sub], cwg_ref, cbg_ref, row0, cols))
            act_ref[pl.ds(row0, FFN_RC), cols] = (
                gate * conv(uv_refs[sub], cwv_ref, cbv_ref, row0, cols)).astype(BF16)

    def down_proj(seg):
        for r in range(seg * chunks_per_half, (seg + 1) * chunks_per_half):
            rows = pl.ds(r * FFN_RC, FFN_RC)
            o_ref[rows, :] += jnp.dot(act_ref[rows, :], wd_ref[...], preferred_element_type=F32)

    up_proj(*pieces[0])
    for k, (sub, seg) in enumerate(pieces):
        if k + 1 < len(pieces):
            up_proj(*pieces[k + 1])
        activate(sub, seg)
        if sub == n_sub - 1:
            down_proj(seg)


def _ffn(c2, h2, w_up_bf, conv_w, conv_b, w_down, seq):
    m = c2.shape[0]
    nf = D_FF // FFN_TF
    blocks_per_seq = seq // FFN_TM
    halo_blocks = FFN_TM // FFN_HALO
    ext = FFN_TM + FFN_HALO
    row = lambda i, j: (i, 0)
    gate_col = lambda i, j: (0, j)
    val_col = lambda i, j: (0, nf + j)
    assert FFN_TF % FFN_SUB == 0 and FFN_TM % (FFN_SEGS * FFN_RC) == 0
    vmem = (2 * FFN_TM * D_MODEL * 2 + ext * D_MODEL * 2 + 2 * FFN_TM * D_MODEL * 4
            + 4 * D_MODEL * FFN_TF * 2 + 2 * FFN_TF * D_MODEL * 2
            + 2 * ext * FFN_TF * 4 + FFN_TM * FFN_TF * 2
            + 4 * ext * FFN_SUB * 4 + 6 * FFN_RC * D_MODEL * 4 + (2 << 20))
    return pl.pallas_call(
        functools.partial(_ffn_kernel, blocks_per_seq=blocks_per_seq),
        name="ffn",
        grid=(m // FFN_TM, nf),
        in_specs=[
            pl.BlockSpec((FFN_TM, D_MODEL), row),
            pl.BlockSpec((FFN_HALO, D_MODEL), lambda i, j: (jnp.maximum(i * halo_blocks - 1, 0), 0)),
            pl.BlockSpec(memory_space=pl.ANY),
            pl.BlockSpec((D_MODEL, FFN_TF), gate_col),
            pl.BlockSpec((D_MODEL, FFN_TF), val_col),
            pl.BlockSpec((CONV_WIDTH, FFN_TF), gate_col),
            pl.BlockSpec((CONV_WIDTH, FFN_TF), val_col),
            pl.BlockSpec((1, FFN_TF), gate_col),
            pl.BlockSpec((1, FFN_TF), val_col),
            pl.BlockSpec((FFN_TF, D_MODEL), lambda i, j: (j, 0)),
        ],
        out_specs=pl.BlockSpec((FFN_TM, D_MODEL), row),
        out_shape=jax.ShapeDtypeStruct((m, D_MODEL), F32),
        scratch_shapes=[
            pltpu.VMEM((ext, D_MODEL), BF16),
            *[pltpu.VMEM((ext, FFN_SUB), F32)] * (2 * (FFN_TF // FFN_SUB)),
            pltpu.VMEM((FFN_TM, FFN_TF), BF16),
            pltpu.SemaphoreType.DMA(()),
        ],
        compiler_params=pltpu.CompilerParams(
            dimension_semantics=("parallel", "arbitrary"), vmem_limit_bytes=_vmem_limit(vmem)),
    )(c2, c2, h2, w_up_bf, w_up_bf, conv_w, conv_w, conv_b, conv_b, w_down)


def _rope_tables(seq):
    inv = jnp.exp(-math.log(ROPE_THETA) * jnp.arange(0, HEAD_DIM, 2, dtype=F32) / HEAD_DIM)
    ang = jnp.arange(seq, dtype=F32)[:, None] * inv[None, :]
    cos, sin = jnp.cos(ang), jnp.sin(ang)
    return jnp.tile(cos, (1, 4)), jnp.tile(jnp.concatenate([-sin, sin], axis=1), (1, 2))


def kernel(x, norm1_g, w_in, b_gate, q_norm_g, k_norm_g, lambda_q1, lambda_k1, lambda_q2, lambda_k2, subln_g, sgu_norm_g, sgu_norm_b, sgu_w, sgu_b, w_att_out, w_sgu_out, w_out, norm2_g, w_up, conv_w, conv_b, w_down):
    bsz, seq, d = x.shape
    assert d == D_MODEL and w_in.shape[0] == 1
    assert seq % IN_TM == 0 and seq % FFN_TM == 0 and seq % ATT_TQ == 0 and seq % MRG_TM == 0
    m = bsz * seq
    x2 = x.reshape(m, d)
    cos_t, sin_t = _rope_tables(seq)
    seg = (jnp.arange(256)[:, None] // HEAD_DIM == jnp.arange(256)[None, :] // HEAD_DIM).astype(BF16)
    tile_heads = lambda g: jnp.tile(g, (1, SEC // HEAD_DIM))
    bs_full = jnp.repeat(sgu_b[0].T, GROUP_DIM, axis=1)

    z2 = _in_proj(x2, norm1_g, w_in[0].astype(BF16), b_gate, tile_heads(q_norm_g), tile_heads(k_norm_g),
                  cos_t, sin_t, seg, sgu_norm_g, sgu_norm_b, seq)
    o3, w_up_bf, w_down_bf, w_att_bf, w_sgu_bf, w_out_bf = _attention(
        z2.reshape(bsz, seq, N_SEC * SEC), lambda_q1, lambda_k1, lambda_q2, lambda_k2, subln_g,
        w_up[0], w_down[0], (w_att_out[0], w_sgu_out[0], w_out[0]))
    h2, c2 = _merge(x2, o3.reshape(m, SEC), z2, sgu_w[0], bs_full, w_att_bf, w_sgu_bf, w_out_bf, norm2_g)
    out = _ffn(c2, h2, w_up_bf, conv_w[0], conv_b, w_down_bf, seq)
    return out.reshape(bsz, seq, d)
```

```python
import functools
import math

import jax
import jax.numpy as jnp
from jax import lax
from jax.experimental import pallas as pl
from jax.experimental.pallas import tpu as pltpu

F32 = jnp.float32
BF16 = jnp.bfloat16

D_MODEL = 2048
N_HEADS = 8
HEAD_DIM = 64
V_DIM = 2 * HEAD_DIM
SEC = 1024
N_SEC = 9
N_GATE_SEC = 4
Z_GATT, Z_GSGU, Z_Q, Z_K, Z_V, Z_U, Z_SV = 0, 2, 4, 5, 6, 7, 8
ROPE_THETA = 10000.0
CHUNK = 128
N_GROUPS = 8
GROUP_DIM = 128
D_FF = 5632
CONV_WIDTH = 3
EPS = 1e-6
LAM_INIT = 0.8 - 0.6 * math.exp(-0.3 * 0)
SQRT_HALF = math.sqrt(0.5)
LOG2_E = math.log2(math.e)

V7X_VMEM_BYTES = 64 * 1024 * 1024
LANES = 128
BF16_SUBLANES = 16

IN_TM = 1024
IN_RC = 256
CAST_ROWS = 256
ATT_TQ = 256
ATT_RC = 16
MRG_TM = 256
FFN_TM = 1024
FFN_TF = 512
FFN_SUB = 256
FFN_SEGS = 2
FFN_RC = 256
FFN_HALO = BF16_SUBLANES


def _gelu(x):
    return 0.5 * x * (1.0 + lax.erf(x * SQRT_HALF))


def _vmem_limit(nbytes):
    return int(min(V7X_VMEM_BYTES - (4 << 20), nbytes))


def _in_proj_kernel(x_ref, g1_ref, w_ref, bg_ref, qg_ref, kg_ref, cos_ref, sin_ref, seg_ref,
                    lng_ref, lnb_ref, o_ref, a_ref):
    j = pl.program_id(1)
    n_chunks = IN_TM // IN_RC

    @pl.when(j == 0)
    def _():
        for r in range(n_chunks):
            rows = pl.ds(r * IN_RC, IN_RC)
            xs = x_ref[rows, :]
            ms = jnp.mean(xs * xs, axis=-1, keepdims=True)
            a_ref[rows, :] = (xs * lax.rsqrt(ms + EPS) * g1_ref[...]).astype(BF16)

    def for_chunks(epilogue):
        for r in range(n_chunks):
            rows = pl.ds(r * IN_RC, IN_RC)
            z = jnp.dot(a_ref[rows, :], w_ref[...], preferred_element_type=F32)
            o_ref[rows, :] = epilogue(z, rows).astype(BF16)

    def qk_epilogue(gain_ref, scale):
        def ep(z, rows):
            sq = (z * z).astype(BF16)
            ss = jnp.concatenate(
                [jnp.dot(sq[:, c * 256:(c + 1) * 256], seg_ref[...], preferred_element_type=F32)
                 for c in range(SEC // 256)], axis=1)
            rinv = lax.rsqrt(ss * (1.0 / HEAD_DIM) + EPS)
            zg = z * gain_ref[...]
            lane = lax.broadcasted_iota(jnp.int32, zg.shape, 1)
            partner = jnp.where((lane & 32) == 0,
                                pltpu.roll(zg, SEC - 32, axis=1), pltpu.roll(zg, 32, axis=1))
            cos = jnp.concatenate([cos_ref[rows, :]] * N_HEADS, axis=1)
            sin = jnp.concatenate([sin_ref[rows, :]] * N_HEADS, axis=1)
            out = (zg * cos + partner * sin) * rinv
            return out * scale if scale != 1.0 else out
        return ep

    @pl.when(j == 0)
    def _():
        for_chunks(qk_epilogue(qg_ref, HEAD_DIM ** -0.5 * LOG2_E))

    @pl.when(j == 1)
    def _():
        for_chunks(qk_epilogue(kg_ref, 1.0))

    @pl.when(j == 2)
    def _():
        for_chunks(lambda z, rows: z)

    @pl.when(j == 3)
    def _():
        for_chunks(lambda z, rows: _gelu(z))

    @pl.when(j == 4)
    def _():
        def ep(z, rows):
            gl = _gelu(z)
            mu = jnp.mean(gl, axis=-1, keepdims=True)
            xc = gl - mu
            var = jnp.mean(xc * xc, axis=-1, keepdims=True)
            return xc * lax.rsqrt(var + EPS) * lng_ref[...] + lnb_ref[...]
        for_chunks(ep)

    @pl.when(j >= 5)
    def _():
        for_chunks(lambda z, rows: jax.nn.sigmoid(z + bg_ref[...]))


def _in_proj(x2, norm1_g, w_in, b_gate, qg_t, kg_t, cos_t, sin_t, seg, ln_g, ln_b, seq):
    m = x2.shape[0]
    blocks_per_seq = seq // IN_TM
    row = lambda i, j: (i, 0)
    const = lambda i, j: (0, 0)
    vmem = (2 * IN_TM * D_MODEL * 4 + 2 * D_MODEL * SEC * 2 + IN_TM * D_MODEL * 2
            + 2 * IN_TM * SEC * 2 + 12 * IN_RC * SEC * 4 + (4 << 20))
    return pl.pallas_call(
        _in_proj_kernel,
        name="in_proj",
        grid=(m // IN_TM, N_SEC),
        in_specs=[
            pl.BlockSpec((IN_TM, D_MODEL), row),
            pl.BlockSpec((1, D_MODEL), const),
            pl.BlockSpec((D_MODEL, SEC), lambda i, j: (0, j)),
            pl.BlockSpec((1, SEC), lambda i, j: (0, jnp.maximum(j - 5, 0))),
            pl.BlockSpec((1, SEC), const),
            pl.BlockSpec((1, SEC), const),
            pl.BlockSpec((IN_TM, LANES), lambda i, j: (i % blocks_per_seq, 0)),
            pl.BlockSpec((IN_TM, LANES), lambda i, j: (i % blocks_per_seq, 0)),
            pl.BlockSpec((256, 256), const),
            pl.BlockSpec((1, SEC), const),
            pl.BlockSpec((1, SEC), const),
        ],
        out_specs=pl.BlockSpec((IN_TM, SEC), lambda i, j: (i, (j + N_GATE_SEC) % N_SEC)),
        out_shape=jax.ShapeDtypeStruct((m, N_SEC * SEC), BF16),
        scratch_shapes=[pltpu.VMEM((IN_TM, D_MODEL), BF16)],
        compiler_params=pltpu.CompilerParams(
            dimension_semantics=("parallel", "arbitrary"), vmem_limit_bytes=_vmem_limit(vmem)),
    )(x2, norm1_g, w_in, b_gate, qg_t, kg_t, cos_t, sin_t, seg, ln_g, ln_b)


def _attn_kernel(lq1_ref, lk1_ref, lq2_ref, lk2_ref, subg_ref, q_ref, k_ref, v_ref, wu_ref, wd_ref,
                 wa_ref, ws_ref, wo_ref, o_ref, wu_bf_ref, wd_bf_ref, wa_bf_ref, ws_bf_ref, wo_bf_ref,
                 s_ref, p_ref, vext_ref, *, seq, merge_cast_steps):
    wu_bf_ref[...] = wu_ref[...].astype(BF16)
    wd_bf_ref[...] = wd_ref[...].astype(BF16)
    step = pl.program_id(0) * N_HEADS + pl.program_id(1)
    for src, dst, (first, n_blocks) in zip((wa_ref, ws_ref, wo_ref), (wa_bf_ref, ws_bf_ref, wo_bf_ref),
                                           merge_cast_steps):
        @pl.when((step >= first) & (step < first + n_blocks))
        def _(src=src, dst=dst):
            dst[...] = src[...].astype(BF16)

    tq = ATT_TQ
    lam = (jnp.exp(jnp.sum(lq1_ref[...] * lk1_ref[...], axis=-1, keepdims=True))
           - jnp.exp(jnp.sum(lq2_ref[...] * lk2_ref[...], axis=-1, keepdims=True)) + LAM_INIT)
    rc = ATT_RC
    lane = lax.broadcasted_iota(jnp.int32, (tq, V_DIM), 1)
    row_i = lax.broadcasted_iota(jnp.int32, (rc, tq), 0)
    col_i = lax.broadcasted_iota(jnp.int32, (rc, tq), 1)
    nt_dims = (((1,), (1,)), ((), ()))

    vext_ref[:, pl.ds(0, V_DIM)] = v_ref[0]
    vext_ref[:, pl.ds(V_DIM, V_DIM)] = jnp.ones((seq, V_DIM), BF16)

    def scores(qi, part):
        off = qi * tq
        q = q_ref[0, pl.ds(off, tq), :]
        qm = jnp.where((lane < HEAD_DIM) == (part == 0), q, jnp.zeros_like(q))
        s_ref[qi % 2, pl.ds(part * tq, tq), pl.ds(0, off + tq)] = lax.dot_general(
            qm, k_ref[0, pl.ds(0, off + tq), :], nt_dims, preferred_element_type=F32)

    order = list(range(seq // tq - 1, -1, -1))
    scores(order[0], 0)
    scores(order[0], 1)
    for pos, qi in enumerate(order):
        slot = qi % 2
        off = qi * tq
        kv_len = off + tq
        on = []
        for part in range(2):
            if pos + 1 < len(order):
                scores(order[pos + 1], part)
            for r in range(part * tq // rc, (part + 1) * tq // rc):
                rows = pl.ds(r * rc, rc)
                keep = ((r * rc) % tq + row_i) >= col_i
                sd = jnp.where(keep, s_ref[slot, rows, pl.ds(off, tq)], -jnp.inf)
                m = jnp.max(sd, axis=-1, keepdims=True)
                if off > 0:
                    so = s_ref[slot, rows, pl.ds(0, off)]
                    m = jnp.maximum(m, jnp.max(so, axis=-1, keepdims=True))
                    p_ref[slot, rows, pl.ds(0, off)] = jnp.exp2(so - m).astype(BF16)
                p_ref[slot, rows, pl.ds(off, tq)] = jnp.exp2(sd - m).astype(BF16)
            ol = jnp.dot(p_ref[slot, pl.ds(part * tq, tq), pl.ds(0, kv_len)], vext_ref[pl.ds(0, kv_len), :],
                         preferred_element_type=F32)
            on.append(ol[:, :V_DIM] / ol[:, V_DIM:])
        o = on[0] - lam * on[1]
        o = o * lax.rsqrt(jnp.mean(o * o, axis=-1, keepdims=True) + EPS) * subg_ref[...]
        o_ref[0, pl.ds(off, tq), :] = (o * (1.0 - LAM_INIT)).astype(BF16)


def _attention(z3, lq1, lk1, lq2, lk2, subln_g, w_up, w_down, merge_weights):
    b, seq, _ = z3.shape
    n_up_tiles = 2 * D_FF // FFN_TF
    n_cast = 2 * n_up_tiles
    dn_rows = D_FF // n_cast
    assert D_FF % n_cast == 0 and dn_rows % BF16_SUBLANES == 0
    cast_id = lambda bi, h: jnp.minimum(bi * N_HEADS + h, n_cast - 1)
    merge_specs, merge_cast_steps, first = [], [], n_cast
    for w in merge_weights:
        n_blocks = w.shape[0] // CAST_ROWS
        index = functools.partial(
            lambda bi, h, first, n_blocks: (jnp.clip(bi * N_HEADS + h - first, 0, n_blocks - 1), 0),
            first=first, n_blocks=n_blocks)
        merge_specs.append(pl.BlockSpec((CAST_ROWS, w.shape[1]), index))
        merge_cast_steps.append((first, n_blocks))
        first += n_blocks
    assert first <= b * N_HEADS
    small = lambda n: pl.BlockSpec((1, n), lambda bi, h: (0, 0))
    blk = lambda off: pl.BlockSpec((1, seq, V_DIM), lambda bi, h: (bi, 0, off + h))
    return pl.pallas_call(
        functools.partial(_attn_kernel, seq=seq, merge_cast_steps=tuple(merge_cast_steps)),
        name="attn",
        grid=(b, N_HEADS),
        in_specs=[small(HEAD_DIM)] * 4 + [small(V_DIM), blk(Z_Q * N_HEADS), blk(Z_K * N_HEADS), blk(Z_V * N_HEADS),
                  pl.BlockSpec((D_MODEL // 2, FFN_TF), lambda bi, h: (cast_id(bi, h) % 2, cast_id(bi, h) // 2)),
                  pl.BlockSpec((dn_rows, D_MODEL), lambda bi, h: (cast_id(bi, h), 0)), *merge_specs],
        out_specs=[pl.BlockSpec((1, seq, V_DIM), lambda bi, h: (bi, 0, h)),
                   pl.BlockSpec((D_MODEL // 2, FFN_TF), lambda bi, h: (cast_id(bi, h) % 2, cast_id(bi, h) // 2)),
                   pl.BlockSpec((dn_rows, D_MODEL), lambda bi, h: (cast_id(bi, h), 0)), *merge_specs],
        out_shape=[jax.ShapeDtypeStruct((b, seq, N_HEADS * V_DIM), BF16),
                   jax.ShapeDtypeStruct((D_MODEL, 2 * D_FF), BF16),
                   jax.ShapeDtypeStruct((D_FF, D_MODEL), BF16),
                   *[jax.ShapeDtypeStruct(w.shape, BF16) for w in merge_weights]],
        scratch_shapes=[pltpu.VMEM((2, 2 * ATT_TQ, seq), F32), pltpu.VMEM((2, 2 * ATT_TQ, seq), BF16),
                        pltpu.VMEM((seq, 2 * V_DIM), BF16)],
        compiler_params=pltpu.CompilerParams(
            dimension_semantics=("arbitrary", "arbitrary"),
            vmem_limit_bytes=_vmem_limit(2 * 2 * ATT_TQ * seq * (4 + 2) + seq * 2 * V_DIM * 2
                                         + 2 * 4 * seq * V_DIM * 2
                                         + 2 * (D_MODEL // 2 * FFN_TF + dn_rows * D_MODEL) * (4 + 2)
                                         + 2 * len(merge_weights) * CAST_ROWS * D_MODEL * (4 + 2) + (16 << 20))),
    )(lq1, lk1, lq2, lk2, subln_g, z3, z3, z3, w_up, w_down, *merge_weights)


def _merge_kernel(x_ref, o_ref, u_ref, sv_ref, ga_ref, gs_ref, wsp_ref, bsp_ref, wa_ref, ws_ref, wo_ref,
                  g2_ref, h_ref, c_ref, sgu_ref):
    t_idx = lax.broadcasted_iota(jnp.int32, (CHUNK, CHUNK), 0)
    p_idx = lax.broadcasted_iota(jnp.int32, (CHUNK, CHUNK), 1)
    for g in range(N_GROUPS):
        cols = slice(g * GROUP_DIM, (g + 1) * GROUP_DIM)
        w = jnp.where(t_idx >= p_idx, wsp_ref[g], 0.0).astype(BF16)
        for c in range(MRG_TM // CHUNK):
            rows = pl.ds(c * CHUNK, CHUNK)
            s = jnp.dot(w, sv_ref[rows, cols], preferred_element_type=F32) + bsp_ref[:, cols]
            sgu_ref[rows, cols] = (u_ref[rows, cols].astype(F32) * s).astype(BF16)
    att_b = jnp.dot(o_ref[...], wa_ref[...], preferred_element_type=F32)
    sgu_b = jnp.dot(sgu_ref[...], ws_ref[...], preferred_element_type=F32)
    mix = (ga_ref[...].astype(F32) * att_b + gs_ref[...].astype(F32) * sgu_b).astype(BF16)
    h = x_ref[...] + jnp.dot(mix, wo_ref[...], preferred_element_type=F32)
    h_ref[...] = h
    ms = jnp.mean(h * h, axis=-1, keepdims=True)
    c_ref[...] = (h * lax.rsqrt(ms + EPS) * g2_ref[...]).astype(BF16)


def _merge(x2, o2, z2, sgu_w, bs_full, w_att_out, w_sgu_out, w_out, norm2_g):
    m = x2.shape[0]
    row = lambda i: (i, 0)
    const2 = lambda i: (0, 0)
    resident = functools.partial(pl.BlockSpec, pipeline_mode=pl.Buffered(1))
    zsec = lambda first, n: pl.BlockSpec((MRG_TM, n * SEC), lambda i: (i, first // n))
    vmem = ((SEC * D_MODEL * 2 + D_MODEL * D_MODEL) * 2
            + 2 * MRG_TM * (D_MODEL * 4 + 3 * SEC * 2 + 2 * D_MODEL * 2 + D_MODEL * 4 + D_MODEL * 2)
            + 6 * MRG_TM * D_MODEL * 4 + (6 << 20))
    return pl.pallas_call(
        _merge_kernel,
        name="merge",
        grid=(m // MRG_TM,),
        in_specs=[
            pl.BlockSpec((MRG_TM, D_MODEL), row),
            pl.BlockSpec((MRG_TM, SEC), row),
            zsec(Z_U, 1), zsec(Z_SV, 1), zsec(Z_GATT, 2), zsec(Z_GSGU, 2),
            resident((N_GROUPS, CHUNK, CHUNK), lambda i: (0, 0, 0)),
            resident((CHUNK, SEC), const2),
            resident((SEC, D_MODEL), const2),
            resident((SEC, D_MODEL), const2),
            resident((D_MODEL, D_MODEL), const2),
            pl.BlockSpec((1, D_MODEL), const2),
        ],
        out_specs=[pl.BlockSpec((MRG_TM, D_MODEL), row), pl.BlockSpec((MRG_TM, D_MODEL), row)],
        out_shape=[jax.ShapeDtypeStruct((m, D_MODEL), F32), jax.ShapeDtypeStruct((m, D_MODEL), BF16)],
        scratch_shapes=[pltpu.VMEM((MRG_TM, SEC), BF16)],
        compiler_params=pltpu.CompilerParams(
            dimension_semantics=("parallel",), vmem_limit_bytes=_vmem_limit(vmem)),
    )(x2, o2, z2, z2, z2, z2, sgu_w, bs_full, w_att_out, w_sgu_out, w_out, norm2_g)


def _ffn_kernel(c_ref, halo_ref, h_hbm, wg_ref, wv_ref, cwg_ref, cwv_ref, cbg_ref, cbv_ref, wd_ref,
                o_ref, cext_ref, *scratch, blocks_per_seq):
    n_sub = FFN_TF // FFN_SUB
    ug_refs, uv_refs, (act_ref, h_sem) = scratch[:n_sub], scratch[n_sub:2 * n_sub], scratch[2 * n_sub:]
    i = pl.program_id(0)
    j = pl.program_id(1)

    h_copy = pltpu.make_async_copy(h_hbm.at[pl.ds(i * FFN_TM, FFN_TM), :], o_ref, h_sem)

    def conv(u_ref, w_ref, b_ref, row0, cols):
        acc = b_ref[:, cols] + w_ref[pl.ds(CONV_WIDTH - 1, 1), cols] * u_ref[pl.ds(FFN_HALO + row0, FFN_RC), :]
        for tap in range(CONV_WIDTH - 1):
            shift = CONV_WIDTH - 1 - tap
            acc = acc + w_ref[pl.ds(tap, 1), cols] * u_ref[pl.ds(FFN_HALO + row0 - shift, FFN_RC), :]
        return acc

    seg_rows = FFN_TM // FFN_SEGS
    chunks_per_half = seg_rows // FFN_RC
    pieces = [(sub, seg) for sub in range(n_sub) for seg in range(FFN_SEGS)]

    def up_proj(sub, seg):
        cols = slice(sub * FFN_SUB, (sub + 1) * FFN_SUB)
        ext_rows = pl.ds(0, FFN_HALO + seg_rows) if seg == 0 else pl.ds(FFN_HALO + seg * seg_rows, seg_rows)
        ug_refs[sub][ext_rows, :] = jnp.dot(cext_ref[ext_rows, :], wg_ref[:, cols], preferred_element_type=F32)
        uv_refs[sub][ext_rows, :] = jnp.dot(cext_ref[ext_rows, :], wv_ref[:, cols], preferred_element_type=F32)

    def activate(sub, seg):
        cols = slice(sub * FFN_SUB, (sub + 1) * FFN_SUB)
        for r in range(seg * chunks_per_half, (seg + 1) * chunks_per_half):
            row0 = r * FFN_RC
            gate = _gelu(conv(ug_refs[sub], cwg_ref, cbg_ref, row0, cols))
            act_ref[pl.ds(row0, FFN_RC), cols] = (
                gate * conv(uv_refs[sub], cwv_ref, cbv_ref, row0, cols)).astype(BF16)

    def down_proj(seg):
        for r in range(seg * chunks_per_half, (seg + 1) * chunks_per_half):
            rows = pl.ds(r * FFN_RC, FFN_RC)
            o_ref[rows, :] += jnp.dot(act_ref[rows, :], wd_ref[...], preferred_element_type=F32)

    def step_body(first_step):
        up_proj(*pieces[0])
        waited = not first_step
        for k, (sub, seg) in enumerate(pieces):
            if k + 1 < len(pieces):
                up_proj(*pieces[k + 1])
            activate(sub, seg)
            if sub == n_sub - 1:
                if not waited:
                    h_copy.wait()
                    waited = True
                down_proj(seg)

    @pl.when(j == 0)
    def _():
        h_copy.start()
        halo = halo_ref[...]
        cext_ref[pl.ds(0, FFN_HALO), :] = jnp.where(i % blocks_per_seq == 0, jnp.zeros_like(halo), halo)
        cext_ref[pl.ds(FFN_HALO, FFN_TM), :] = c_ref[...]
        step_body(first_step=True)

    @pl.when(j > 0)
    def _():
        step_body(first_step=False)


def _ffn(c2, h2, w_up_bf, conv_w, conv_b, w_down, seq):
    m = c2.shape[0]
    nf = D_FF // FFN_TF
    blocks_per_seq = seq // FFN_TM
    halo_blocks = FFN_TM // FFN_HALO
    ext = FFN_TM + FFN_HALO
    row = lambda i, j: (i, 0)
    gate_col = lambda i, j: (0, j)
    val_col = lambda i, j: (0, nf + j)
    assert FFN_TF % FFN_SUB == 0 and FFN_TM % (FFN_SEGS * FFN_RC) == 0
    vmem = (2 * FFN_TM * D_MODEL * 2 + ext * D_MODEL * 2 + 2 * FFN_TM * D_MODEL * 4
            + 4 * D_MODEL * FFN_TF * 2 + 2 * FFN_TF * D_MODEL * 2
            + 2 * ext * FFN_TF * 4 + FFN_TM * FFN_TF * 2
            + 4 * ext * FFN_SUB * 4 + 6 * FFN_RC * D_MODEL * 4 + (2 << 20))
    return pl.pallas_call(
        functools.partial(_ffn_kernel, blocks_per_seq=blocks_per_seq),
        name="ffn",
        grid=(m // FFN_TM, nf),
        in_specs=[
            pl.BlockSpec((FFN_TM, D_MODEL), row),
            pl.BlockSpec((FFN_HALO, D_MODEL), lambda i, j: (jnp.maximum(i * halo_blocks - 1, 0), 0)),
            pl.BlockSpec(memory_space=pl.ANY),
            pl.BlockSpec((D_MODEL, FFN_TF), gate_col),
            pl.BlockSpec((D_MODEL, FFN_TF), val_col),
            pl.BlockSpec((CONV_WIDTH, FFN_TF), gate_col),
            pl.BlockSpec((CONV_WIDTH, FFN_TF), val_col),
            pl.BlockSpec((1, FFN_TF), gate_col),
            pl.BlockSpec((1, FFN_TF), val_col),
            pl.BlockSpec((FFN_TF, D_MODEL), lambda i, j: (j, 0)),
        ],
        out_specs=pl.BlockSpec((FFN_TM, D_MODEL), row),
        out_shape=jax.ShapeDtypeStruct((m, D_MODEL), F32),
        scratch_shapes=[
            pltpu.VMEM((ext, D_MODEL), BF16),
            *[pltpu.VMEM((ext, FFN_SUB), F32)] * (2 * (FFN_TF // FFN_SUB)),
            pltpu.VMEM((FFN_TM, FFN_TF), BF16),
            pltpu.SemaphoreType.DMA(()),
        ],
        compiler_params=pltpu.CompilerParams(
            dimension_semantics=("parallel", "arbitrary"), vmem_limit_bytes=_vmem_limit(vmem)),
    )(c2, c2, h2, w_up_bf, w_up_bf, conv_w, conv_w, conv_b, conv_b, w_down)


def _rope_tables(seq):
    inv = jnp.exp(-math.log(ROPE_THETA) * jnp.arange(0, HEAD_DIM, 2, dtype=F32) / HEAD_DIM)
    ang = jnp.arange(seq, dtype=F32)[:, None] * inv[None, :]
    cos, sin = jnp.cos(ang), jnp.sin(ang)
    return jnp.tile(cos, (1, 4)), jnp.tile(jnp.concatenate([-sin, sin], axis=1), (1, 2))


def kernel(x, norm1_g, w_in, b_gate, q_norm_g, k_norm_g, lambda_q1, lambda_k1, lambda_q2, lambda_k2, subln_g, sgu_norm_g, sgu_norm_b, sgu_w, sgu_b, w_att_out, w_sgu_out, w_out, norm2_g, w_up, conv_w, conv_b, w_down):
    bsz, seq, d = x.shape
    assert d == D_MODEL and w_in.shape[0] == 1
    assert seq % IN_TM == 0 and seq % FFN_TM == 0 and seq % ATT_TQ == 0 and seq % MRG_TM == 0
    m = bsz * seq
    x2 = x.reshape(m, d)
    cos_t, sin_t = _rope_tables(seq)
    seg = (jnp.arange(256)[:, None] // HEAD_DIM == jnp.arange(256)[None, :] // HEAD_DIM).astype(BF16)
    tile_heads = lambda g: jnp.tile(g, (1, SEC // HEAD_DIM))
    bs_full = jnp.repeat(sgu_b[0].T, GROUP_DIM, axis=1)

    z2 = _in_proj(x2, norm1_g, w_in[0].astype(BF16), b_gate, tile_heads(q_norm_g), tile_heads(k_norm_g),
                  cos_t, sin_t, seg, sgu_norm_g, sgu_norm_b, seq)
    o3, w_up_bf, w_down_bf, w_att_bf, w_sgu_bf, w_out_bf = _attention(
        z2.reshape(bsz, seq, N_SEC * SEC), lambda_q1, lambda_k1, lambda_q2, lambda_k2, subln_g,
        w_up[0], w_down[0], (w_att_out[0], w_sgu_out[0], w_out[0]))
    h2, c2 = _merge(x2, o3.reshape(m, SEC), z2, sgu_w[0], bs_full, w_att_bf, w_sgu_bf, w_out_bf, norm2_g)
    out = _ffn(c2, h2, w_up_bf, conv_w[0], conv_b, w_down_bf, seq)
    return out.reshape(bsz, seq, d)
```

```python
import functools
import math

import jax
import jax.numpy as jnp
from jax import lax
from jax.experimental import pallas as pl
from jax.experimental.pallas import tpu as pltpu

F32 = jnp.float32
BF16 = jnp.bfloat16

D_MODEL = 2048
N_HEADS = 8
HEAD_DIM = 64
V_DIM = 2 * HEAD_DIM
SEC = 1024
N_SEC = 9
N_GATE_SEC = 4
Z_GATT, Z_GSGU, Z_Q, Z_K, Z_V, Z_U, Z_SV = 0, 2, 4, 5, 6, 7, 8
ROPE_THETA = 10000.0
CHUNK = 128
N_GROUPS = 8
GROUP_DIM = 128
D_FF = 5632
CONV_WIDTH = 3
EPS = 1e-6
LAM_INIT = 0.8 - 0.6 * math.exp(-0.3 * 0)
SQRT_HALF = math.sqrt(0.5)
LOG2_E = math.log2(math.e)

V7X_VMEM_BYTES = 64 * 1024 * 1024
LANES = 128
BF16_SUBLANES = 16

IN_TM = 1024
IN_RC = 256
CAST_ROWS = 256
ATT_TQ = 256
ATT_RC = 16
MRG_TM = 256
FFN_TM = 1024
FFN_TF = 512
FFN_SUB = 256
FFN_SEGS = 2
FFN_RC = 256
FFN_HALO = BF16_SUBLANES


def _gelu(x):
    return 0.5 * x * (1.0 + lax.erf(x * SQRT_HALF))


def _vmem_limit(nbytes):
    return int(min(V7X_VMEM_BYTES - (4 << 20), nbytes))


def _in_proj_kernel(x_ref, g1_ref, w_ref, bg_ref, qkg_ref, cos_ref, sin_ref, seg_ref,
                    lng_ref, lnb_ref, o_ref, a_ref):
    j = pl.program_id(1)
    n_chunks = IN_TM // IN_RC

    @pl.when(j == 0)
    def _():
        for r in range(n_chunks):
            rows = pl.ds(r * IN_RC, IN_RC)
            xs = x_ref[rows, :]
            ms = jnp.mean(xs * xs, axis=-1, keepdims=True)
            a_ref[rows, :] = (xs * lax.rsqrt(ms + EPS) * g1_ref[...]).astype(BF16)

    def for_chunks(epilogue):
        for r in range(n_chunks):
            rows = pl.ds(r * IN_RC, IN_RC)
            z = jnp.dot(a_ref[rows, :], w_ref[...], preferred_element_type=F32)
            o_ref[rows, :] = epilogue(z, rows).astype(BF16)

    def qk_epilogue(z, rows):
        sq = (z * z).astype(BF16)
        ss = jnp.concatenate(
            [jnp.dot(sq[:, c * 256:(c + 1) * 256], seg_ref[...], preferred_element_type=F32)
             for c in range(SEC // 256)], axis=1)
        rinv = lax.rsqrt(ss * (1.0 / HEAD_DIM) + EPS)
        zg = z * qkg_ref[...]
        lane = lax.broadcasted_iota(jnp.int32, zg.shape, 1)
        partner = jnp.where((lane & 32) == 0,
                            pltpu.roll(zg, SEC - 32, axis=1), pltpu.roll(zg, 32, axis=1))
        cos = jnp.concatenate([cos_ref[rows, :]] * N_HEADS, axis=1)
        sin = jnp.concatenate([sin_ref[rows, :]] * N_HEADS, axis=1)
        return (zg * cos + partner * sin) * rinv

    @pl.when(j < 2)
    def _():
        for_chunks(qk_epilogue)

    @pl.when(j == 2)
    def _():
        for_chunks(lambda z, rows: z)

    @pl.when(j == 3)
    def _():
        for_chunks(lambda z, rows: _gelu(z))

    @pl.when(j == 4)
    def _():
        def ep(z, rows):
            gl = _gelu(z)
            mu = jnp.mean(gl, axis=-1, keepdims=True)
            xc = gl - mu
            var = jnp.mean(xc * xc, axis=-1, keepdims=True)
            return xc * lax.rsqrt(var + EPS) * lng_ref[...] + lnb_ref[...]
        for_chunks(ep)

    @pl.when(j >= 5)
    def _():
        for_chunks(lambda z, rows: jax.nn.sigmoid(z + bg_ref[...]))


def _in_proj(x2, norm1_g, w_in, b_gate, qk_gain, cos_t, sin_t, seg, ln_g, ln_b, seq):
    m = x2.shape[0]
    blocks_per_seq = seq // IN_TM
    row = lambda i, j: (i, 0)
    const = lambda i, j: (0, 0)
    vmem = (2 * IN_TM * D_MODEL * 4 + 2 * D_MODEL * SEC * 2 + IN_TM * D_MODEL * 2
            + 2 * IN_TM * SEC * 2 + 12 * IN_RC * SEC * 4 + (4 << 20))
    return pl.pallas_call(
        _in_proj_kernel,
        name="in_proj",
        grid=(m // IN_TM, N_SEC),
        in_specs=[
            pl.BlockSpec((IN_TM, D_MODEL), row),
            pl.BlockSpec((1, D_MODEL), const),
            pl.BlockSpec((D_MODEL, SEC), lambda i, j: (0, j)),
            pl.BlockSpec((1, SEC), lambda i, j: (0, jnp.maximum(j - 5, 0))),
            pl.BlockSpec((None, 1, SEC), lambda i, j: (jnp.minimum(j, 1), 0, 0)),
            pl.BlockSpec((IN_TM, LANES), lambda i, j: (i % blocks_per_seq, 0)),
            pl.BlockSpec((IN_TM, LANES), lambda i, j: (i % blocks_per_seq, 0)),
            pl.BlockSpec((256, 256), const),
            pl.BlockSpec((1, SEC), const),
            pl.BlockSpec((1, SEC), const),
        ],
        out_specs=pl.BlockSpec((IN_TM, SEC), lambda i, j: (i, (j + N_GATE_SEC) % N_SEC)),
        out_shape=jax.ShapeDtypeStruct((m, N_SEC * SEC), BF16),
        scratch_shapes=[pltpu.VMEM((IN_TM, D_MODEL), BF16)],
        compiler_params=pltpu.CompilerParams(
            dimension_semantics=("parallel", "arbitrary"), vmem_limit_bytes=_vmem_limit(vmem)),
    )(x2, norm1_g, w_in, b_gate, qk_gain, cos_t, sin_t, seg, ln_g, ln_b)


def _attn_kernel(lq1_ref, lk1_ref, lq2_ref, lk2_ref, subg_ref, q_ref, k_ref, v_ref, wu_ref, wd_ref,
                 wa_ref, ws_ref, wo_ref, o_ref, wu_bf_ref, wd_bf_ref, wa_bf_ref, ws_bf_ref, wo_bf_ref,
                 s_ref, p_ref, vext_ref, *, seq, merge_cast_steps):
    wu_bf_ref[...] = wu_ref[...].astype(BF16)
    wd_bf_ref[...] = wd_ref[...].astype(BF16)
    step = pl.program_id(0) * N_HEADS + pl.program_id(1)
    for src, dst, (first, n_blocks) in zip((wa_ref, ws_ref, wo_ref), (wa_bf_ref, ws_bf_ref, wo_bf_ref),
                                           merge_cast_steps):
        @pl.when((step >= first) & (step < first + n_blocks))
        def _(src=src, dst=dst):
            dst[...] = src[...].astype(BF16)

    tq = ATT_TQ
    lam = (jnp.exp(jnp.sum(lq1_ref[...] * lk1_ref[...], axis=-1, keepdims=True))
           - jnp.exp(jnp.sum(lq2_ref[...] * lk2_ref[...], axis=-1, keepdims=True)) + LAM_INIT)
    rc = ATT_RC
    lane = lax.broadcasted_iota(jnp.int32, (tq, V_DIM), 1)
    row_i = lax.broadcasted_iota(jnp.int32, (rc, tq), 0)
    col_i = lax.broadcasted_iota(jnp.int32, (rc, tq), 1)
    nt_dims = (((1,), (1,)), ((), ()))

    vext_ref[:, pl.ds(0, V_DIM)] = v_ref[0]
    vext_ref[:, pl.ds(V_DIM, V_DIM)] = jnp.ones((seq, V_DIM), BF16)

    def scores(qi, part):
        off = qi * tq
        q = q_ref[0, pl.ds(off, tq), :]
        qm = jnp.where((lane < HEAD_DIM) == (part == 0), q, jnp.zeros_like(q))
        s_ref[qi % 2, pl.ds(part * tq, tq), pl.ds(0, off + tq)] = lax.dot_general(
            qm, k_ref[0, pl.ds(0, off + tq), :], nt_dims, preferred_element_type=F32)

    order = list(range(seq // tq - 1, -1, -1))
    scores(order[0], 0)
    scores(order[0], 1)
    for pos, qi in enumerate(order):
        slot = qi % 2
        off = qi * tq
        kv_len = off + tq
        on = []
        for part in range(2):
            if pos + 1 < len(order):
                scores(order[pos + 1], part)
            for r in range(part * tq // rc, (part + 1) * tq // rc):
                rows = pl.ds(r * rc, rc)
                keep = ((r * rc) % tq + row_i) >= col_i
                sd = jnp.where(keep, s_ref[slot, rows, pl.ds(off, tq)], -jnp.inf)
                m = jnp.max(sd, axis=-1, keepdims=True)
                if off > 0:
                    so = s_ref[slot, rows, pl.ds(0, off)]
                    m = jnp.maximum(m, jnp.max(so, axis=-1, keepdims=True))
                    p_ref[slot, rows, pl.ds(0, off)] = jnp.exp2(so - m).astype(BF16)
                p_ref[slot, rows, pl.ds(off, tq)] = jnp.exp2(sd - m).astype(BF16)
            ol = jnp.dot(p_ref[slot, pl.ds(part * tq, tq), pl.ds(0, kv_len)], vext_ref[pl.ds(0, kv_len), :],
                         preferred_element_type=F32)
            on.append(ol[:, :V_DIM] / ol[:, V_DIM:])
        o = on[0] - lam * on[1]
        o = o * lax.rsqrt(jnp.mean(o * o, axis=-1, keepdims=True) + EPS) * subg_ref[...]
        o_ref[0, pl.ds(off, tq), :] = (o * (1.0 - LAM_INIT)).astype(BF16)


def _attention(z3, lq1, lk1, lq2, lk2, subln_g, w_up, w_down, merge_weights):
    b, seq, _ = z3.shape
    n_up_tiles = 2 * D_FF // FFN_TF
    n_cast = 2 * n_up_tiles
    dn_rows = D_FF // n_cast
    assert D_FF % n_cast == 0 and dn_rows % BF16_SUBLANES == 0
    cast_id = lambda bi, h: jnp.minimum(bi * N_HEADS + h, n_cast - 1)
    merge_specs, merge_cast_steps, first = [], [], n_cast
    for w in merge_weights:
        n_blocks = w.shape[0] // CAST_ROWS
        index = functools.partial(
            lambda bi, h, first, n_blocks: (jnp.clip(bi * N_HEADS + h - first, 0, n_blocks - 1), 0),
            first=first, n_blocks=n_blocks)
        merge_specs.append(pl.BlockSpec((CAST_ROWS, w.shape[1]), index))
        merge_cast_steps.append((first, n_blocks))
        first += n_blocks
    assert first <= b * N_HEADS
    small = lambda n: pl.BlockSpec((1, n), lambda bi, h: (0, 0))
    blk = lambda off: pl.BlockSpec((1, seq, V_DIM), lambda bi, h: (bi, 0, off + h))
    return pl.pallas_call(
        functools.partial(_attn_kernel, seq=seq, merge_cast_steps=tuple(merge_cast_steps)),
        name="attn",
        grid=(b, N_HEADS),
        in_specs=[small(HEAD_DIM)] * 4 + [small(V_DIM), blk(Z_Q * N_HEADS), blk(Z_K * N_HEADS), blk(Z_V * N_HEADS),
                  pl.BlockSpec((D_MODEL // 2, FFN_TF), lambda bi, h: (cast_id(bi, h) % 2, cast_id(bi, h) // 2)),
                  pl.BlockSpec((dn_rows, D_MODEL), lambda bi, h: (cast_id(bi, h), 0)), *merge_specs],
        out_specs=[pl.BlockSpec((1, seq, V_DIM), lambda bi, h: (bi, 0, h)),
                   pl.BlockSpec((D_MODEL // 2, FFN_TF), lambda bi, h: (cast_id(bi, h) % 2, cast_id(bi, h) // 2)),
                   pl.BlockSpec((dn_rows, D_MODEL), lambda bi, h: (cast_id(bi, h), 0)), *merge_specs],
        out_shape=[jax.ShapeDtypeStruct((b, seq, N_HEADS * V_DIM), BF16),
                   jax.ShapeDtypeStruct((D_MODEL, 2 * D_FF), BF16),
                   jax.ShapeDtypeStruct((D_FF, D_MODEL), BF16),
                   *[jax.ShapeDtypeStruct(w.shape, BF16) for w in merge_weights]],
        scratch_shapes=[pltpu.VMEM((2, 2 * ATT_TQ, seq), F32), pltpu.VMEM((2, 2 * ATT_TQ, seq), BF16),
                        pltpu.VMEM((seq, 2 * V_DIM), BF16)],
        compiler_params=pltpu.CompilerParams(
            dimension_semantics=("arbitrary", "arbitrary"),
            vmem_limit_bytes=_vmem_limit(2 * 2 * ATT_TQ * seq * (4 + 2) + seq * 2 * V_DIM * 2
                                         + 2 * 4 * seq * V_DIM * 2
                                         + 2 * (D_MODEL // 2 * FFN_TF + dn_rows * D_MODEL) * (4 + 2)
                                         + 2 * len(merge_weights) * CAST_ROWS * D_MODEL * (4 + 2) + (16 << 20))),
    )(lq1, lk1, lq2, lk2, subln_g, z3, z3, z3, w_up, w_down, *merge_weights)


def _merge_kernel(x_ref, o_ref, u_ref, sv_ref, ga_ref, gs_ref, wsp_ref, bsp_ref, wa_ref, ws_ref, wo_ref,
                  g2_ref, h_ref, c_ref, sgu_ref):
    t_idx = lax.broadcasted_iota(jnp.int32, (CHUNK, CHUNK), 0)
    p_idx = lax.broadcasted_iota(jnp.int32, (CHUNK, CHUNK), 1)
    for g in range(N_GROUPS):
        cols = slice(g * GROUP_DIM, (g + 1) * GROUP_DIM)
        w = jnp.where(t_idx >= p_idx, wsp_ref[g], 0.0).astype(BF16)
        for c in range(MRG_TM // CHUNK):
            rows = pl.ds(c * CHUNK, CHUNK)
            s = jnp.dot(w, sv_ref[rows, cols], preferred_element_type=F32) + bsp_ref[:, cols]
            sgu_ref[rows, cols] = (u_ref[rows, cols].astype(F32) * s).astype(BF16)
    att_b = jnp.dot(o_ref[...], wa_ref[...], preferred_element_type=F32)
    sgu_b = jnp.dot(sgu_ref[...], ws_ref[...], preferred_element_type=F32)
    mix = (ga_ref[...].astype(F32) * att_b + gs_ref[...].astype(F32) * sgu_b).astype(BF16)
    h = x_ref[...] + jnp.dot(mix, wo_ref[...], preferred_element_type=F32)
    h_ref[...] = h
    ms = jnp.mean(h * h, axis=-1, keepdims=True)
    c_ref[...] = (h * lax.rsqrt(ms + EPS) * g2_ref[...]).astype(BF16)


def _merge(x2, o2, z2, sgu_w, bs_full, w_att_out, w_sgu_out, w_out, norm2_g):
    m = x2.shape[0]
    row = lambda i: (i, 0)
    const2 = lambda i: (0, 0)
    resident = functools.partial(pl.BlockSpec, pipeline_mode=pl.Buffered(1))
    zsec = lambda first, n: pl.BlockSpec((MRG_TM, n * SEC), lambda i: (i, first // n))
    vmem = ((SEC * D_MODEL * 2 + D_MODEL * D_MODEL) * 2
            + 2 * MRG_TM * (D_MODEL * 4 + 3 * SEC * 2 + 2 * D_MODEL * 2 + D_MODEL * 4 + D_MODEL * 2)
            + 6 * MRG_TM * D_MODEL * 4 + (6 << 20))
    return pl.pallas_call(
        _merge_kernel,
        name="merge",
        grid=(m // MRG_TM,),
        in_specs=[
            pl.BlockSpec((MRG_TM, D_MODEL), row),
            pl.BlockSpec((MRG_TM, SEC), row),
            zsec(Z_U, 1), zsec(Z_SV, 1), zsec(Z_GATT, 2), zsec(Z_GSGU, 2),
            resident((N_GROUPS, CHUNK, CHUNK), lambda i: (0, 0, 0)),
            resident((CHUNK, SEC), const2),
            resident((SEC, D_MODEL), const2),
            resident((SEC, D_MODEL), const2),
            resident((D_MODEL, D_MODEL), const2),
            pl.BlockSpec((1, D_MODEL), const2),
        ],
        out_specs=[pl.BlockSpec((MRG_TM, D_MODEL), row), pl.BlockSpec((MRG_TM, D_MODEL), row)],
        out_shape=[jax.ShapeDtypeStruct((m, D_MODEL), F32), jax.ShapeDtypeStruct((m, D_MODEL), BF16)],
        scratch_shapes=[pltpu.VMEM((MRG_TM, SEC), BF16)],
        compiler_params=pltpu.CompilerParams(
            dimension_semantics=("parallel",), vmem_limit_bytes=_vmem_limit(vmem)),
    )(x2, o2, z2, z2, z2, z2, sgu_w, bs_full, w_att_out, w_sgu_out, w_out, norm2_g)


def _ffn_kernel(c_ref, halo_ref, h_hbm, wg_ref, wv_ref, cwg_ref, cwv_ref, cbg_ref, cbv_ref, wd_ref,
                o_ref, cext_ref, *scratch, blocks_per_seq):
    n_sub = FFN_TF // FFN_SUB
    ug_refs, uv_refs, (act_ref, h_sem) = scratch[:n_sub], scratch[n_sub:2 * n_sub], scratch[2 * n_sub:]
    i = pl.program_id(0)
    j = pl.program_id(1)

    @pl.when(j == 0)
    def _():
        h_copy = pltpu.make_async_copy(h_hbm.at[pl.ds(i * FFN_TM, FFN_TM), :], o_ref, h_sem)
        h_copy.start()
        halo = halo_ref[...]
        cext_ref[pl.ds(0, FFN_HALO), :] = jnp.where(i % blocks_per_seq == 0, jnp.zeros_like(halo), halo)
        cext_ref[pl.ds(FFN_HALO, FFN_TM), :] = c_ref[...]
        h_copy.wait()

    def conv(u_ref, w_ref, b_ref, row0, cols):
        acc = b_ref[:, cols] + w_ref[pl.ds(CONV_WIDTH - 1, 1), cols] * u_ref[pl.ds(FFN_HALO + row0, FFN_RC), :]
        for tap in range(CONV_WIDTH - 1):
            shift = CONV_WIDTH - 1 - tap
            acc = acc + w_ref[pl.ds(tap, 1), cols] * u_ref[pl.ds(FFN_HALO + row0 - shift, FFN_RC), :]
        return acc

    seg_rows = FFN_TM // FFN_SEGS
    chunks_per_half = seg_rows // FFN_RC
    pieces = [(sub, seg) for sub in range(n_sub) for seg in range(FFN_SEGS)]

    def up_proj(sub, seg):
        cols = slice(sub * FFN_SUB, (sub + 1) * FFN_SUB)
        ext_rows = pl.ds(0, FFN_HALO + seg_rows) if seg == 0 else pl.ds(FFN_HALO + seg * seg_rows, seg_rows)
        ug_refs[sub][ext_rows, :] = jnp.dot(cext_ref[ext_rows, :], wg_ref[:, cols], preferred_element_type=F32)
        uv_refs[sub][ext_rows, :] = jnp.dot(cext_ref[ext_rows, :], wv_ref[:, cols], preferred_element_type=F32)

    def activate(sub, seg):
        cols = slice(sub * FFN_SUB, (sub + 1) * FFN_SUB)
        for r in range(seg * chunks_per_half, (seg + 1) * chunks_per_half):
            row0 = r * FFN_RC
            gate = _gelu(conv(ug_refs[sub], cwg_ref, cbg_ref, row0, cols))
            act_ref[pl.ds(row0, FFN_RC), cols] = (
                gate * conv(uv_refs[sub], cwv_ref, cbv_ref, row0, cols)).astype(BF16)

    def down_proj(seg):
        for r in range(seg * chunks_per_half, (seg + 1) * chunks_per_half):
            rows = pl.ds(r * FFN_RC, FFN_RC)
            o_ref[rows, :] += jnp.dot(act_ref[rows, :], wd_ref[...], preferred_element_type=F32)

    up_proj(*pieces[0])
    for k, (sub, seg) in enumerate(pieces):
        if k + 1 < len(pieces):
            up_proj(*pieces[k + 1])
        activate(sub, seg)
        if sub == n_sub - 1:
            down_proj(seg)


def _ffn(c2, h2, w_up_bf, conv_w, conv_b, w_down, seq):
    m = c2.shape[0]
    nf = D_FF // FFN_TF
    blocks_per_seq = seq // FFN_TM
    halo_blocks = FFN_TM // FFN_HALO
    ext = FFN_TM + FFN_HALO
    row = lambda i, j: (i, 0)
    gate_col = lambda i, j: (0, j)
    val_col = lambda i, j: (0, nf + j)
    assert FFN_TF % FFN_SUB == 0 and FFN_TM % (FFN_SEGS * FFN_RC) == 0
    vmem = (2 * FFN_TM * D_MODEL * 2 + ext * D_MODEL * 2 + 2 * FFN_TM * D_MODEL * 4
            + 4 * D_MODEL * FFN_TF * 2 + 2 * FFN_TF * D_MODEL * 2
            + 2 * ext * FFN_TF * 4 + FFN_TM * FFN_TF * 2
            + 4 * ext * FFN_SUB * 4 + 6 * FFN_RC * D_MODEL * 4 + (2 << 20))
    return pl.pallas_call(
        functools.partial(_ffn_kernel, blocks_per_seq=blocks_per_seq),
        name="ffn",
        grid=(m // FFN_TM, nf),
        in_specs=[
            pl.BlockSpec((FFN_TM, D_MODEL), row),
            pl.BlockSpec((FFN_HALO, D_MODEL), lambda i, j: (jnp.maximum(i * halo_blocks - 1, 0), 0)),
            pl.BlockSpec(memory_space=pl.ANY),
            pl.BlockSpec((D_MODEL, FFN_TF), gate_col),
            pl.BlockSpec((D_MODEL, FFN_TF), val_col),
            pl.BlockSpec((CONV_WIDTH, FFN_TF), gate_col),
            pl.BlockSpec((CONV_WIDTH, FFN_TF), val_col),
            pl.BlockSpec((1, FFN_TF), gate_col),
            pl.BlockSpec((1, FFN_TF), val_col),
            pl.BlockSpec((FFN_TF, D_MODEL), lambda i, j: (j, 0)),
        ],
        out_specs=pl.BlockSpec((FFN_TM, D_MODEL), row),
        out_shape=jax.ShapeDtypeStruct((m, D_MODEL), F32),
        scratch_shapes=[
            pltpu.VMEM((ext, D_MODEL), BF16),
            *[pltpu.VMEM((ext, FFN_SUB), F32)] * (2 * (FFN_TF // FFN_SUB)),
            pltpu.VMEM((FFN_TM, FFN_TF), BF16),
            pltpu.SemaphoreType.DMA(()),
        ],
        compiler_params=pltpu.CompilerParams(
            dimension_semantics=("parallel", "arbitrary"), vmem_limit_bytes=_vmem_limit(vmem)),
    )(c2, c2, h2, w_up_bf, w_up_bf, conv_w, conv_w, conv_b, conv_b, w_down)


def _rope_tables(seq):
    inv = jnp.exp(-math.log(ROPE_THETA) * jnp.arange(0, HEAD_DIM, 2, dtype=F32) / HEAD_DIM)
    ang = jnp.arange(seq, dtype=F32)[:, None] * inv[None, :]
    cos, sin = jnp.cos(ang), jnp.sin(ang)
    return jnp.tile(cos, (1, 4)), jnp.tile(jnp.concatenate([-sin, sin], axis=1), (1, 2))


def kernel(x, norm1_g, w_in, b_gate, q_norm_g, k_norm_g, lambda_q1, lambda_k1, lambda_q2, lambda_k2, subln_g, sgu_norm_g, sgu_norm_b, sgu_w, sgu_b, w_att_out, w_sgu_out, w_out, norm2_g, w_up, conv_w, conv_b, w_down):
    bsz, seq, d = x.shape
    assert d == D_MODEL and w_in.shape[0] == 1
    assert seq % IN_TM == 0 and seq % FFN_TM == 0 and seq % ATT_TQ == 0 and seq % MRG_TM == 0
    m = bsz * seq
    x2 = x.reshape(m, d)
    cos_t, sin_t = _rope_tables(seq)
    seg = (jnp.arange(256)[:, None] // HEAD_DIM == jnp.arange(256)[None, :] // HEAD_DIM).astype(BF16)
    tile_heads = lambda g: jnp.tile(g, (1, SEC // HEAD_DIM))
    bs_full = jnp.repeat(sgu_b[0].T, GROUP_DIM, axis=1)

    qk_gain = jnp.stack([tile_heads(q_norm_g) * (HEAD_DIM ** -0.5 * LOG2_E), tile_heads(k_norm_g)])
    z2 = _in_proj(x2, norm1_g, w_in[0].astype(BF16), b_gate, qk_gain,
                  cos_t, sin_t, seg, sgu_norm_g, sgu_norm_b, seq)
    o3, w_up_bf, w_down_bf, w_att_bf, w_sgu_bf, w_out_bf = _attention(
        z2.reshape(bsz, seq, N_SEC * SEC), lambda_q1, lambda_k1, lambda_q2, lambda_k2, subln_g,
        w_up[0], w_down[0], (w_att_out[0], w_sgu_out[0], w_out[0]))
    h2, c2 = _merge(x2, o3.reshape(m, SEC), z2, sgu_w[0], bs_full, w_att_bf, w_sgu_bf, w_out_bf, norm2_g)
    out = _ffn(c2, h2, w_up_bf, conv_w[0], conv_b, w_down_bf, seq)
    return out.reshape(bsz, seq, d)
```

```python
import functools
import math

import jax
import jax.numpy as jnp
from jax import lax
from jax.experimental import pallas as pl
from jax.experimental.pallas import tpu as pltpu

F32 = jnp.float32
BF16 = jnp.bfloat16

D_MODEL = 2048
N_HEADS = 8
HEAD_DIM = 64
V_DIM = 2 * HEAD_DIM
SEC = 1024
N_SEC = 9
N_GATE_SEC = 4
Z_GATT, Z_GSGU, Z_Q, Z_K, Z_V, Z_U, Z_SV = 0, 2, 4, 5, 6, 7, 8
ROPE_THETA = 10000.0
CHUNK = 128
N_GROUPS = 8
GROUP_DIM = 128
D_FF = 5632
CONV_WIDTH = 3
EPS = 1e-6
LAM_INIT = 0.8 - 0.6 * math.exp(-0.3 * 0)
SQRT_HALF = math.sqrt(0.5)
LOG2_E = math.log2(math.e)

V7X_VMEM_BYTES = 64 * 1024 * 1024
LANES = 128
BF16_SUBLANES = 16

IN_TM = 1024
IN_RC = 256
IN_WBUFS = 3
CAST_ROWS = 256
ATT_TQ = 256
ATT_RC = 16
MRG_TM = 256
FFN_TM = 1024
FFN_TF = 512
FFN_SUB = 256
FFN_SEGS = 2
FFN_RC = 256
FFN_HALO = BF16_SUBLANES


def _gelu(x):
    return 0.5 * x * (1.0 + lax.erf(x * SQRT_HALF))


def _vmem_limit(nbytes):
    return int(min(V7X_VMEM_BYTES - (4 << 20), nbytes))


def _in_proj_kernel(x_ref, g1_ref, w_hbm, bg_ref, qkg_ref, cos_ref, sin_ref, seg_ref,
                    lng_ref, lnb_ref, o_ref, a_ref, wbuf_ref, w_sem):
    j = pl.program_id(1)
    n_chunks = IN_TM // IN_RC

    t = pl.program_id(0) * N_SEC + j
    n_steps = pl.num_programs(0) * N_SEC

    def tile_copy(step, slot):
        col = pl.multiple_of((step % N_SEC) * SEC, SEC)
        return pltpu.make_async_copy(w_hbm.at[:, pl.ds(col, SEC)], wbuf_ref.at[slot], w_sem.at[slot])

    @pl.when(t == 0)
    def _():
        for ahead in range(IN_WBUFS - 1):
            tile_copy(ahead, ahead).start()

    @pl.when(t + (IN_WBUFS - 1) < n_steps)
    def _():
        tile_copy(t + (IN_WBUFS - 1), (t + (IN_WBUFS - 1)) % IN_WBUFS).start()

    tile_copy(t, t % IN_WBUFS).wait()
    w_ref = wbuf_ref.at[t % IN_WBUFS]

    @pl.when(j == 0)
    def _():
        for r in range(n_chunks):
            rows = pl.ds(r * IN_RC, IN_RC)
            xs = x_ref[rows, :]
            ms = jnp.mean(xs * xs, axis=-1, keepdims=True)
            a_ref[rows, :] = (xs * lax.rsqrt(ms + EPS) * g1_ref[...]).astype(BF16)

    def for_chunks(epilogue):
        for r in range(n_chunks):
            rows = pl.ds(r * IN_RC, IN_RC)
            z = jnp.dot(a_ref[rows, :], w_ref[...], preferred_element_type=F32)
            o_ref[rows, :] = epilogue(z, rows).astype(BF16)

    def qk_epilogue(z, rows):
        sq = (z * z).astype(BF16)
        ss = jnp.concatenate(
            [jnp.dot(sq[:, c * 256:(c + 1) * 256], seg_ref[...], preferred_element_type=F32)
             for c in range(SEC // 256)], axis=1)
        rinv = lax.rsqrt(ss * (1.0 / HEAD_DIM) + EPS)
        zg = z * qkg_ref[...]
        lane = lax.broadcasted_iota(jnp.int32, zg.shape, 1)
        partner = jnp.where((lane & 32) == 0,
                            pltpu.roll(zg, SEC - 32, axis=1), pltpu.roll(zg, 32, axis=1))
        cos = jnp.concatenate([cos_ref[rows, :]] * N_HEADS, axis=1)
        sin = jnp.concatenate([sin_ref[rows, :]] * N_HEADS, axis=1)
        return (zg * cos + partner * sin) * rinv

    @pl.when(j < 2)
    def _():
        for_chunks(qk_epilogue)

    @pl.when(j == 2)
    def _():
        for_chunks(lambda z, rows: z)

    @pl.when(j == 3)
    def _():
        for_chunks(lambda z, rows: _gelu(z))

    @pl.when(j == 4)
    def _():
        def ep(z, rows):
            gl = _gelu(z)
            mu = jnp.mean(gl, axis=-1, keepdims=True)
            xc = gl - mu
            var = jnp.mean(xc * xc, axis=-1, keepdims=True)
            return xc * lax.rsqrt(var + EPS) * lng_ref[...] + lnb_ref[...]
        for_chunks(ep)

    @pl.when(j >= 5)
    def _():
        for_chunks(lambda z, rows: jax.nn.sigmoid(z + bg_ref[...]))


def _in_proj(x2, norm1_g, w_in, b_gate, qk_gain, cos_t, sin_t, seg, ln_g, ln_b, seq):
    m = x2.shape[0]
    blocks_per_seq = seq // IN_TM
    row = lambda i, j: (i, 0)
    const = lambda i, j: (0, 0)
    vmem = (2 * IN_TM * D_MODEL * 4 + IN_WBUFS * D_MODEL * SEC * 2 + IN_TM * D_MODEL * 2
            + 2 * IN_TM * SEC * 2 + 12 * IN_RC * SEC * 4 + (4 << 20))
    return pl.pallas_call(
        _in_proj_kernel,
        name="in_proj",
        grid=(m // IN_TM, N_SEC),
        in_specs=[
            pl.BlockSpec((IN_TM, D_MODEL), row),
            pl.BlockSpec((1, D_MODEL), const),
            pl.BlockSpec(memory_space=pl.ANY),
            pl.BlockSpec((1, SEC), lambda i, j: (0, jnp.maximum(j - 5, 0))),
            pl.BlockSpec((None, 1, SEC), lambda i, j: (jnp.minimum(j, 1), 0, 0)),
            pl.BlockSpec((IN_TM, LANES), lambda i, j: (i % blocks_per_seq, 0)),
            pl.BlockSpec((IN_TM, LANES), lambda i, j: (i % blocks_per_seq, 0)),
            pl.BlockSpec((256, 256), const),
            pl.BlockSpec((1, SEC), const),
            pl.BlockSpec((1, SEC), const),
        ],
        out_specs=pl.BlockSpec((IN_TM, SEC), lambda i, j: (i, (j + N_GATE_SEC) % N_SEC)),
        out_shape=jax.ShapeDtypeStruct((m, N_SEC * SEC), BF16),
        scratch_shapes=[pltpu.VMEM((IN_TM, D_MODEL), BF16), pltpu.VMEM((IN_WBUFS, D_MODEL, SEC), BF16),
                        pltpu.SemaphoreType.DMA((IN_WBUFS,))],
        compiler_params=pltpu.CompilerParams(
            dimension_semantics=("arbitrary", "arbitrary"), vmem_limit_bytes=_vmem_limit(vmem)),
    )(x2, norm1_g, w_in, b_gate, qk_gain, cos_t, sin_t, seg, ln_g, ln_b)


def _attn_kernel(lq1_ref, lk1_ref, lq2_ref, lk2_ref, subg_ref, q_ref, k_ref, v_ref, wu_ref, wd_ref,
                 wa_ref, ws_ref, wo_ref, o_ref, wu_bf_ref, wd_bf_ref, wa_bf_ref, ws_bf_ref, wo_bf_ref,
                 s_ref, p_ref, vext_ref, *, seq, merge_cast_steps):
    wu_bf_ref[...] = wu_ref[...].astype(BF16)
    wd_bf_ref[...] = wd_ref[...].astype(BF16)
    step = pl.program_id(0) * N_HEADS + pl.program_id(1)
    for src, dst, (first, n_blocks) in zip((wa_ref, ws_ref, wo_ref), (wa_bf_ref, ws_bf_ref, wo_bf_ref),
                                           merge_cast_steps):
        @pl.when((step >= first) & (step < first + n_blocks))
        def _(src=src, dst=dst):
            dst[...] = src[...].astype(BF16)

    tq = ATT_TQ
    lam = (jnp.exp(jnp.sum(lq1_ref[...] * lk1_ref[...], axis=-1, keepdims=True))
           - jnp.exp(jnp.sum(lq2_ref[...] * lk2_ref[...], axis=-1, keepdims=True)) + LAM_INIT)
    rc = ATT_RC
    lane = lax.broadcasted_iota(jnp.int32, (tq, V_DIM), 1)
    row_i = lax.broadcasted_iota(jnp.int32, (rc, tq), 0)
    col_i = lax.broadcasted_iota(jnp.int32, (rc, tq), 1)
    nt_dims = (((1,), (1,)), ((), ()))

    vext_ref[:, pl.ds(0, V_DIM)] = v_ref[0]
    vext_ref[:, pl.ds(V_DIM, V_DIM)] = jnp.ones((seq, V_DIM), BF16)

    def scores(qi, part):
        off = qi * tq
        q = q_ref[0, pl.ds(off, tq), :]
        qm = jnp.where((lane < HEAD_DIM) == (part == 0), q, jnp.zeros_like(q))
        s_ref[qi % 2, pl.ds(part * tq, tq), pl.ds(0, off + tq)] = lax.dot_general(
            qm, k_ref[0, pl.ds(0, off + tq), :], nt_dims, preferred_element_type=F32)

    order = list(range(seq // tq - 1, -1, -1))
    scores(order[0], 0)
    scores(order[0], 1)
    for pos, qi in enumerate(order):
        slot = qi % 2
        off = qi * tq
        kv_len = off + tq
        on = []
        for part in range(2):
            if pos + 1 < len(order):
                scores(order[pos + 1], part)
            for r in range(part * tq // rc, (part + 1) * tq // rc):
                rows = pl.ds(r * rc, rc)
                keep = ((r * rc) % tq + row_i) >= col_i
                sd = jnp.where(keep, s_ref[slot, rows, pl.ds(off, tq)], -jnp.inf)
                m = jnp.max(sd, axis=-1, keepdims=True)
                if off > 0:
                    so = s_ref[slot, rows, pl.ds(0, off)]
                    m = jnp.maximum(m, jnp.max(so, axis=-1, keepdims=True))
                    p_ref[slot, rows, pl.ds(0, off)] = jnp.exp2(so - m).astype(BF16)
                p_ref[slot, rows, pl.ds(off, tq)] = jnp.exp2(sd - m).astype(BF16)
            ol = jnp.dot(p_ref[slot, pl.ds(part * tq, tq), pl.ds(0, kv_len)], vext_ref[pl.ds(0, kv_len), :],
                         preferred_element_type=F32)
            on.append(ol[:, :V_DIM] / ol[:, V_DIM:])
        o = on[0] - lam * on[1]
        o = o * lax.rsqrt(jnp.mean(o * o, axis=-1, keepdims=True) + EPS) * subg_ref[...]
        o_ref[0, pl.ds(off, tq), :] = (o * (1.0 - LAM_INIT)).astype(BF16)


def _attention(z3, lq1, lk1, lq2, lk2, subln_g, w_up, w_down, merge_weights):
    b, seq, _ = z3.shape
    n_up_tiles = 2 * D_FF // FFN_TF
    n_cast = 2 * n_up_tiles
    dn_rows = D_FF // n_cast
    assert D_FF % n_cast == 0 and dn_rows % BF16_SUBLANES == 0
    cast_id = lambda bi, h: jnp.minimum(bi * N_HEADS + h, n_cast - 1)
    merge_specs, merge_cast_steps, first = [], [], n_cast
    for w in merge_weights:
        n_blocks = w.shape[0] // CAST_ROWS
        index = functools.partial(
            lambda bi, h, first, n_blocks: (jnp.clip(bi * N_HEADS + h - first, 0, n_blocks - 1), 0),
            first=first, n_blocks=n_blocks)
        merge_specs.append(pl.BlockSpec((CAST_ROWS, w.shape[1]), index))
        merge_cast_steps.append((first, n_blocks))
        first += n_blocks
    assert first <= b * N_HEADS
    small = lambda n: pl.BlockSpec((1, n), lambda bi, h: (0, 0))
    blk = lambda off: pl.BlockSpec((1, seq, V_DIM), lambda bi, h: (bi, 0, off + h))
    return pl.pallas_call(
        functools.partial(_attn_kernel, seq=seq, merge_cast_steps=tuple(merge_cast_steps)),
        name="attn",
        grid=(b, N_HEADS),
        in_specs=[small(HEAD_DIM)] * 4 + [small(V_DIM), blk(Z_Q * N_HEADS), blk(Z_K * N_HEADS), blk(Z_V * N_HEADS),
                  pl.BlockSpec((D_MODEL // 2, FFN_TF), lambda bi, h: (cast_id(bi, h) % 2, cast_id(bi, h) // 2)),
                  pl.BlockSpec((dn_rows, D_MODEL), lambda bi, h: (cast_id(bi, h), 0)), *merge_specs],
        out_specs=[pl.BlockSpec((1, seq, V_DIM), lambda bi, h: (bi, 0, h)),
                   pl.BlockSpec((D_MODEL // 2, FFN_TF), lambda bi, h: (cast_id(bi, h) % 2, cast_id(bi, h) // 2)),
                   pl.BlockSpec((dn_rows, D_MODEL), lambda bi, h: (cast_id(bi, h), 0)), *merge_specs],
        out_shape=[jax.ShapeDtypeStruct((b, seq, N_HEADS * V_DIM), BF16),
                   jax.ShapeDtypeStruct((D_MODEL, 2 * D_FF), BF16),
                   jax.ShapeDtypeStruct((D_FF, D_MODEL), BF16),
                   *[jax.ShapeDtypeStruct(w.shape, BF16) for w in merge_weights]],
        scratch_shapes=[pltpu.VMEM((2, 2 * ATT_TQ, seq), F32), pltpu.VMEM((2, 2 * ATT_TQ, seq), BF16),
                        pltpu.VMEM((seq, 2 * V_DIM), BF16)],
        compiler_params=pltpu.CompilerParams(
            dimension_semantics=("arbitrary", "arbitrary"),
            vmem_limit_bytes=_vmem_limit(2 * 2 * ATT_TQ * seq * (4 + 2) + seq * 2 * V_DIM * 2
                                         + 2 * 4 * seq * V_DIM * 2
                                         + 2 * (D_MODEL // 2 * FFN_TF + dn_rows * D_MODEL) * (4 + 2)
                                         + 2 * len(merge_weights) * CAST_ROWS * D_MODEL * (4 + 2) + (16 << 20))),
    )(lq1, lk1, lq2, lk2, subln_g, z3, z3, z3, w_up, w_down, *merge_weights)


def _merge_kernel(x_ref, o_ref, u_ref, sv_ref, ga_ref, gs_ref, wsp_ref, bsp_ref, wa_ref, ws_ref, wo_ref,
                  g2_ref, h_ref, c_ref, sgu_ref):
    t_idx = lax.broadcasted_iota(jnp.int32, (CHUNK, CHUNK), 0)
    p_idx = lax.broadcasted_iota(jnp.int32, (CHUNK, CHUNK), 1)
    for g in range(N_GROUPS):
        cols = slice(g * GROUP_DIM, (g + 1) * GROUP_DIM)
        w = jnp.where(t_idx >= p_idx, wsp_ref[g], 0.0).astype(BF16)
        for c in range(MRG_TM // CHUNK):
            rows = pl.ds(c * CHUNK, CHUNK)
            s = jnp.dot(w, sv_ref[rows, cols], preferred_element_type=F32) + bsp_ref[:, cols]
            sgu_ref[rows, cols] = (u_ref[rows, cols].astype(F32) * s).astype(BF16)
    att_b = jnp.dot(o_ref[...], wa_ref[...], preferred_element_type=F32)
    sgu_b = jnp.dot(sgu_ref[...], ws_ref[...], preferred_element_type=F32)
    mix = (ga_ref[...].astype(F32) * att_b + gs_ref[...].astype(F32) * sgu_b).astype(BF16)
    h = x_ref[...] + jnp.dot(mix, wo_ref[...], preferred_element_type=F32)
    h_ref[...] = h
    ms = jnp.mean(h * h, axis=-1, keepdims=True)
    c_ref[...] = (h * lax.rsqrt(ms + EPS) * g2_ref[...]).astype(BF16)


def _merge(x2, o2, z2, sgu_w, bs_full, w_att_out, w_sgu_out, w_out, norm2_g):
    m = x2.shape[0]
    row = lambda i: (i, 0)
    const2 = lambda i: (0, 0)
    resident = functools.partial(pl.BlockSpec, pipeline_mode=pl.Buffered(1))
    zsec = lambda first, n: pl.BlockSpec((MRG_TM, n * SEC), lambda i: (i, first // n))
    vmem = ((SEC * D_MODEL * 2 + D_MODEL * D_MODEL) * 2
            + 2 * MRG_TM * (D_MODEL * 4 + 3 * SEC * 2 + 2 * D_MODEL * 2 + D_MODEL * 4 + D_MODEL * 2)
            + 6 * MRG_TM * D_MODEL * 4 + (6 << 20))
    return pl.pallas_call(
        _merge_kernel,
        name="merge",
        grid=(m // MRG_TM,),
        in_specs=[
            pl.BlockSpec((MRG_TM, D_MODEL), row),
            pl.BlockSpec((MRG_TM, SEC), row),
            zsec(Z_U, 1), zsec(Z_SV, 1), zsec(Z_GATT, 2), zsec(Z_GSGU, 2),
            resident((N_GROUPS, CHUNK, CHUNK), lambda i: (0, 0, 0)),
            resident((CHUNK, SEC), const2),
            resident((SEC, D_MODEL), const2),
            resident((SEC, D_MODEL), const2),
            resident((D_MODEL, D_MODEL), const2),
            pl.BlockSpec((1, D_MODEL), const2),
        ],
        out_specs=[pl.BlockSpec((MRG_TM, D_MODEL), row), pl.BlockSpec((MRG_TM, D_MODEL), row)],
        out_shape=[jax.ShapeDtypeStruct((m, D_MODEL), F32), jax.ShapeDtypeStruct((m, D_MODEL), BF16)],
        scratch_shapes=[pltpu.VMEM((MRG_TM, SEC), BF16)],
        compiler_params=pltpu.CompilerParams(
            dimension_semantics=("parallel",), vmem_limit_bytes=_vmem_limit(vmem)),
    )(x2, o2, z2, z2, z2, z2, sgu_w, bs_full, w_att_out, w_sgu_out, w_out, norm2_g)


def _ffn_kernel(c_ref, halo_ref, h_hbm, wg_ref, wv_ref, cwg_ref, cwv_ref, cbg_ref, cbv_ref, wd_ref,
                o_ref, cext_ref, *scratch, blocks_per_seq):
    n_sub = FFN_TF // FFN_SUB
    ug_refs, uv_refs, (act_ref, h_sem) = scratch[:n_sub], scratch[n_sub:2 * n_sub], scratch[2 * n_sub:]
    i = pl.program_id(0)
    j = pl.program_id(1)

    @pl.when(j == 0)
    def _():
        h_copy = pltpu.make_async_copy(h_hbm.at[pl.ds(i * FFN_TM, FFN_TM), :], o_ref, h_sem)
        h_copy.start()
        halo = halo_ref[...]
        cext_ref[pl.ds(0, FFN_HALO), :] = jnp.where(i % blocks_per_seq == 0, jnp.zeros_like(halo), halo)
        cext_ref[pl.ds(FFN_HALO, FFN_TM), :] = c_ref[...]
        h_copy.wait()

    def conv(u_ref, w_ref, b_ref, row0, cols):
        acc = b_ref[:, cols] + w_ref[pl.ds(CONV_WIDTH - 1, 1), cols] * u_ref[pl.ds(FFN_HALO + row0, FFN_RC), :]
        for tap in range(CONV_WIDTH - 1):
            shift = CONV_WIDTH - 1 - tap
            acc = acc + w_ref[pl.ds(tap, 1), cols] * u_ref[pl.ds(FFN_HALO + row0 - shift, FFN_RC), :]
        return acc

    seg_rows = FFN_TM // FFN_SEGS
    chunks_per_half = seg_rows // FFN_RC
    pieces = [(sub, seg) for sub in range(n_sub) for seg in range(FFN_SEGS)]

    def up_proj(sub, seg):
        cols = slice(sub * FFN_SUB, (sub + 1) * FFN_SUB)
        ext_rows = pl.ds(0, FFN_HALO + seg_rows) if seg == 0 else pl.ds(FFN_HALO + seg * seg_rows, seg_rows)
        ug_refs[sub][ext_rows, :] = jnp.dot(cext_ref[ext_rows, :], wg_ref[:, cols], preferred_element_type=F32)
        uv_refs[sub][ext_rows, :] = jnp.dot(cext_ref[ext_rows, :], wv_ref[:, cols], preferred_element_type=F32)

    def activate(sub, seg):
        cols = slice(sub * FFN_SUB, (sub + 1) * FFN_SUB)
        for r in range(seg * chunks_per_half, (seg + 1) * chunks_per_half):
            row0 = r * FFN_RC
            gate = _gelu(conv(ug_refs[sub], cwg_ref, cbg_ref, row0, cols))
            act_ref[pl.ds(row0, FFN_RC), cols] = (
                gate * conv(uv_refs[sub], cwv_ref, cbv_ref, row0, cols)).astype(BF16)

    def down_proj(seg):
        for r in range(seg * chunks_per_half, (seg + 1) * chunks_per_half):
            rows = pl.ds(r * FFN_RC, FFN_RC)
            o_ref[rows, :] += jnp.dot(act_ref[rows, :], wd_ref[...], preferred_element_type=F32)

    up_proj(*pieces[0])
    for k, (sub, seg) in enumerate(pieces):
        if k + 1 < len(pieces):
            up_proj(*pieces[k + 1])
        activate(sub, seg)
        if sub == n_sub - 1:
            down_proj(seg)


def _ffn(c2, h2, w_up_bf, conv_w, conv_b, w_down, seq):
    m = c2.shape[0]
    nf = D_FF // FFN_TF
    blocks_per_seq = seq // FFN_TM
    halo_blocks = FFN_TM // FFN_HALO
    ext = FFN_TM + FFN_HALO
    row = lambda i, j: (i, 0)
    gate_col = lambda i, j: (0, j)
    val_col = lambda i, j: (0, nf + j)
    assert FFN_TF % FFN_SUB == 0 and FFN_TM % (FFN_SEGS * FFN_RC) == 0
    vmem = (2 * FFN_TM * D_MODEL * 2 + ext * D_MODEL * 2 + 2 * FFN_TM * D_MODEL * 4
            + 4 * D_MODEL * FFN_TF * 2 + 2 * FFN_TF * D_MODEL * 2
            + 2 * ext * FFN_TF * 4 + FFN_TM * FFN_TF * 2
            + 4 * ext * FFN_SUB * 4 + 6 * FFN_RC * D_MODEL * 4 + (2 << 20))
    return pl.pallas_call(
        functools.partial(_ffn_kernel, blocks_per_seq=blocks_per_seq),
        name="ffn",
        grid=(m // FFN_TM, nf),
        in_specs=[
            pl.BlockSpec((FFN_TM, D_MODEL), row),
            pl.BlockSpec((FFN_HALO, D_MODEL), lambda i, j: (jnp.maximum(i * halo_blocks - 1, 0), 0)),
            pl.BlockSpec(memory_space=pl.ANY),
            pl.BlockSpec((D_MODEL, FFN_TF), gate_col),
            pl.BlockSpec((D_MODEL, FFN_TF), val_col),
            pl.BlockSpec((CONV_WIDTH, FFN_TF), gate_col),
            pl.BlockSpec((CONV_WIDTH, FFN_TF), val_col),
            pl.BlockSpec((1, FFN_TF), gate_col),
            pl.BlockSpec((1, FFN_TF), val_col),
            pl.BlockSpec((FFN_TF, D_MODEL), lambda i, j: (j, 0)),
        ],
        out_specs=pl.BlockSpec((FFN_TM, D_MODEL), row),
        out_shape=jax.ShapeDtypeStruct((m, D_MODEL), F32),
        scratch_shapes=[
            pltpu.VMEM((ext, D_MODEL), BF16),
            *[pltpu.VMEM((ext, FFN_SUB), F32)] * (2 * (FFN_TF // FFN_SUB)),
            pltpu.VMEM((FFN_TM, FFN_TF), BF16),
            pltpu.SemaphoreType.DMA(()),
        ],
        compiler_params=pltpu.CompilerParams(
            dimension_semantics=("parallel", "arbitrary"), vmem_limit_bytes=_vmem_limit(vmem)),
    )(c2, c2, h2, w_up_bf, w_up_bf, conv_w, conv_w, conv_b, conv_b, w_down)


def _rope_tables(seq):
    inv = jnp.exp(-math.log(ROPE_THETA) * jnp.arange(0, HEAD_DIM, 2, dtype=F32) / HEAD_DIM)
    ang = jnp.arange(seq, dtype=F32)[:, None] * inv[None, :]
    cos, sin = jnp.cos(ang), jnp.sin(ang)
    return jnp.tile(cos, (1, 4)), jnp.tile(jnp.concatenate([-sin, sin], axis=1), (1, 2))


def kernel(x, norm1_g, w_in, b_gate, q_norm_g, k_norm_g, lambda_q1, lambda_k1, lambda_q2, lambda_k2, subln_g, sgu_norm_g, sgu_norm_b, sgu_w, sgu_b, w_att_out, w_sgu_out, w_out, norm2_g, w_up, conv_w, conv_b, w_down):
    bsz, seq, d = x.shape
    assert d == D_MODEL and w_in.shape[0] == 1
    assert seq % IN_TM == 0 and seq % FFN_TM == 0 and seq % ATT_TQ == 0 and seq % MRG_TM == 0
    m = bsz * seq
    x2 = x.reshape(m, d)
    cos_t, sin_t = _rope_tables(seq)
    seg = (jnp.arange(256)[:, None] // HEAD_DIM == jnp.arange(256)[None, :] // HEAD_DIM).astype(BF16)
    tile_heads = lambda g: jnp.tile(g, (1, SEC // HEAD_DIM))
    bs_full = jnp.repeat(sgu_b[0].T, GROUP_DIM, axis=1)

    qk_gain = jnp.stack([tile_heads(q_norm_g) * (HEAD_DIM ** -0.5 * LOG2_E), tile_heads(k_norm_g)])
    z2 = _in_proj(x2, norm1_g, w_in[0].astype(BF16), b_gate, qk_gain,
                  cos_t, sin_t, seg, sgu_norm_g, sgu_norm_b, seq)
    o3, w_up_bf, w_down_bf, w_att_bf, w_sgu_bf, w_out_bf = _attention(
        z2.reshape(bsz, seq, N_SEC * SEC), lambda_q1, lambda_k1, lambda_q2, lambda_k2, subln_g,
        w_up[0], w_down[0], (w_att_out[0], w_sgu_out[0], w_out[0]))
    h2, c2 = _merge(x2, o3.reshape(m, SEC), z2, sgu_w[0], bs_full, w_att_bf, w_sgu_bf, w_out_bf, norm2_g)
    out = _ffn(c2, h2, w_up_bf, conv_w[0], conv_b, w_down_bf, seq)
    return out.reshape(bsz, seq, d)
```
